```python
import jax, jax.numpy as jnp
from jax import lax
import numpy as np

D_MODEL = 1024
BATCH = 32
SEQ = 2048
DEPTH = 4
DEC_BATCH = 32
DEC_SEQ = 16
PAST_LEN = 1024

CHUNK = 64
N_A_LAYERS = DEPTH // 2
N_B_LAYERS = DEPTH - N_A_LAYERS
CONV_WIDTH = 3
CONV_DIM = D_MODEL
SB_HEADS = 16
SB_HEAD_DIM = D_MODEL // SB_HEADS
Q_BLOCK = 128
PEER_HEADS = 8
PEER_N_KEYS = 128
PEER_N_EXPERTS = PEER_N_KEYS * PEER_N_KEYS
PEER_TOPK = 16
PEER_QDIM = 256
PEER_HALF = PEER_QDIM // 2
PEER_BLOCK = 256
LN_EPS = 1e-5
DEEPNORM_ALPHA = (2.0 * DEPTH) ** 0.25
DEEPNORM_BETA = (8.0 * DEPTH) ** -0.25

kernel_name = "yoco_shortconv_stickbreaking_peer_step"


def _layernorm(x, g, b):
    xf = x.astype(jnp.float32)
    mu = jnp.mean(xf, axis=-1, keepdims=True)
    var = jnp.mean(jnp.square(xf - mu), axis=-1, keepdims=True)
    y = (xf - mu) * lax.rsqrt(var + LN_EPS)
    return (y * g.astype(jnp.float32) + b.astype(jnp.float32)).astype(x.dtype)


def _short_conv_mixer(x, conv_prev, w_in, w_dw, w_out):
    S = x.shape[1]
    b_gate, c_gate, xt = jnp.split(x @ w_in, 3, axis=-1)
    u = c_gate * xt
    up = jnp.concatenate([conv_prev.astype(u.dtype), u], axis=1)
    acc = w_dw[0] * up[:, 0:S]
    for w in range(1, CONV_WIDTH):
        acc = acc + w_dw[w] * up[:, w:w + S]
    y = (b_gate * acc) @ w_out
    return y, up[:, -(CONV_WIDTH - 1):]


def _sb_block(q, k, v, q_pos):
    scale = SB_HEAD_DIM ** -0.5
    z = jnp.einsum("bqhd,bkhd->bhqk", q, k).astype(jnp.float32) * scale
    k_pos = jnp.arange(k.shape[1])
    causal = k_pos[None, :] < q_pos[:, None]
    log_not = jnp.where(causal, jax.nn.log_sigmoid(-z), 0.0)
    suffix = lax.cumsum(log_not, axis=3, reverse=True) - log_not
    a = jnp.where(causal, jnp.exp(jax.nn.log_sigmoid(z) + suffix), 0.0)
    return jnp.einsum("bhqk,bkhd->bqhd", a.astype(v.dtype), v)


def _sb_mixer(x, k_all, v_all, w_q, w_o, q_pos0):
    B, S, _ = x.shape
    q = (x @ w_q).reshape(B, S, SB_HEADS, SB_HEAD_DIM)
    q_pos = q_pos0 + jnp.arange(S)
    if S % Q_BLOCK == 0:
        nb = S // Q_BLOCK
        qb = q.reshape(B, nb, Q_BLOCK, SB_HEADS, SB_HEAD_DIM).transpose(1, 0, 2, 3, 4)
        pb = q_pos.reshape(nb, Q_BLOCK)
        ob = lax.map(lambda a: _sb_block(a[0], k_all, v_all, a[1]), (qb, pb))
        o = ob.transpose(1, 0, 2, 3, 4).reshape(B, S, SB_HEADS * SB_HEAD_DIM)
    else:
        o = _sb_block(q, k_all, v_all, q_pos).reshape(B, S, SB_HEADS * SB_HEAD_DIM)
    return o @ w_o


def _peer(x, w_q, subkeys, u_tab, v_tab):
    B, S, D = x.shape
    T = B * S
    n_blk = -(-T // PEER_BLOCK)
    pad = n_blk * PEER_BLOCK - T
    xp = jnp.pad(x.reshape(T, D), ((0, pad), (0, 0))).reshape(n_blk, PEER_BLOCK, D)
    K = PEER_TOPK

    def block(xb):
        q = (xb @ w_q).reshape(PEER_BLOCK, PEER_HEADS, 2, PEER_HALF).astype(jnp.float32)
        s = jnp.einsum("thpc,hpnc->thpn", q, subkeys.astype(jnp.float32))
        s_top, i_top = lax.top_k(s, K)
        cand = s_top[:, :, 0, :, None] + s_top[:, :, 1, None, :]
        cand_idx = i_top[:, :, 0, :, None] * PEER_N_KEYS + i_top[:, :, 1, None, :]
        best, pos = lax.top_k(cand.reshape(PEER_BLOCK, PEER_HEADS, K * K), K)
        expert = jnp.take_along_axis(cand_idx.reshape(PEER_BLOCK, PEER_HEADS, K * K), pos, axis=-1)
        g = jax.nn.softmax(best, axis=-1)
        act = jax.nn.gelu(jnp.einsum("td,thkd->thk", xb, u_tab[expert]), approximate=False)
        wgt = (g * act.astype(jnp.float32)).astype(xb.dtype)
        return jnp.einsum("thk,thkd->td", wgt, v_tab[expert])

    out = lax.map(block, xp).reshape(n_blk * PEER_BLOCK, D)[:T]
    return out.reshape(B, S, D)


def _trunk(x, conv_prev, k_past, v_past, q_pos0, conv_w_in, conv_w_dw, conv_w_out,
           sb_w_q, sb_w_o, kv_w_k, kv_w_v, peer_w_q, peer_subkeys, peer_u, peer_v, ln_g, ln_b):
    B, S, _ = x.shape
    h = x
    new_conv = []
    k_all = v_all = k_new = v_new = None
    for layer in range(DEPTH):
        if layer < N_A_LAYERS:
            mix, st = _short_conv_mixer(h, conv_prev[layer], conv_w_in[layer],
                                        conv_w_dw[layer], conv_w_out[layer])
            new_conv.append(st)
        else:
            if layer == N_A_LAYERS:
                k_new = (h @ kv_w_k).reshape(B, S, SB_HEADS, SB_HEAD_DIM)
                v_new = (h @ kv_w_v).reshape(B, S, SB_HEADS, SB_HEAD_DIM)
                if k_past is None:
                    k_all, v_all = k_new, v_new
                else:
                    k_all = jnp.concatenate([k_past.astype(k_new.dtype), k_new], axis=1)
                    v_all = jnp.concatenate([v_past.astype(v_new.dtype), v_new], axis=1)
            j = layer - N_A_LAYERS
            mix = _sb_mixer(h, k_all, v_all, sb_w_q[j], sb_w_o[j], q_pos0)
        h = _layernorm(DEEPNORM_ALPHA * h + mix, ln_g[layer, 0], ln_b[layer, 0])
        ff = _peer(h, peer_w_q[layer], peer_subkeys[layer], peer_u[layer], peer_v[layer])
        h = _layernorm(DEEPNORM_ALPHA * h + ff, ln_g[layer, 1], ln_b[layer, 1])
    return h, jnp.stack(new_conv, axis=0), k_new, v_new


def setup_inputs(seed: int = 0) -> dict:
    key = jax.random.key(seed)
    ks = jax.random.split(key, 20)
    f32 = jnp.float32
    HD = SB_HEADS * SB_HEAD_DIM
    nrm = lambda k, shape, s: jax.random.normal(k, shape, f32) * s
    return {
        "x_prompt": nrm(ks[0], (BATCH, SEQ, D_MODEL), 1.0),
        "x_sample": nrm(ks[1], (DEC_BATCH, DEC_SEQ, D_MODEL), 1.0),
        "state_conv": nrm(ks[2], (N_A_LAYERS, DEC_BATCH, CONV_WIDTH - 1, CONV_DIM), 1.0),
        "cache_k": nrm(ks[3], (DEC_BATCH, PAST_LEN, SB_HEADS, SB_HEAD_DIM), 1.0),
        "cache_v": nrm(ks[4], (DEC_BATCH, PAST_LEN, SB_HEADS, SB_HEAD_DIM), DEEPNORM_BETA),
        "conv_w_in": nrm(ks[5], (N_A_LAYERS, D_MODEL, 3 * CONV_DIM), D_MODEL ** -0.5),
        "conv_w_dw": nrm(ks[6], (N_A_LAYERS, CONV_WIDTH, CONV_DIM), CONV_WIDTH ** -0.5),
        "conv_w_out": nrm(ks[7], (N_A_LAYERS, CONV_DIM, D_MODEL), DEEPNORM_BETA * CONV_DIM ** -0.5),
        "sb_w_q": nrm(ks[8], (N_B_LAYERS, D_MODEL, HD), D_MODEL ** -0.5),
        "sb_w_o": nrm(ks[9], (N_B_LAYERS, HD, D_MODEL), DEEPNORM_BETA * HD ** -0.5),
        "kv_w_k": nrm(ks[10], (D_MODEL, HD), D_MODEL ** -0.5),
        "kv_w_v": nrm(ks[11], (D_MODEL, HD), DEEPNORM_BETA * D_MODEL ** -0.5),
        "peer_w_q": nrm(ks[12], (DEPTH, D_MODEL, PEER_HEADS * PEER_QDIM), D_MODEL ** -0.5),
        "peer_subkeys": nrm(ks[13], (DEPTH, PEER_HEADS, 2, PEER_N_KEYS, PEER_HALF), PEER_HALF ** -0.5),
        "peer_u": nrm(ks[14], (DEPTH, PEER_N_EXPERTS, D_MODEL), D_MODEL ** -0.5),
        "peer_v": nrm(ks[15], (DEPTH, PEER_N_EXPERTS, D_MODEL), DEEPNORM_BETA * PEER_HEADS ** -0.5),
        "ln_g": 1.0 + nrm(ks[16], (DEPTH, 2, D_MODEL), 0.02),
        "ln_b": nrm(ks[17], (DEPTH, 2, D_MODEL), 0.02),
    }


def reference(x_prompt, x_sample, state_conv, cache_k, cache_v, conv_w_in, conv_w_dw, conv_w_out,
              sb_w_q, sb_w_o, kv_w_k, kv_w_v, peer_w_q, peer_subkeys, peer_u, peer_v, ln_g, ln_b):
    zero_conv = jnp.zeros((N_A_LAYERS, x_prompt.shape[0], CONV_WIDTH - 1, CONV_DIM), x_prompt.dtype)
    y_prompt, new_conv_prompt, new_k_prompt, new_v_prompt = _trunk(
        x_prompt, zero_conv, None, None, 0, conv_w_in, conv_w_dw, conv_w_out,
        sb_w_q, sb_w_o, kv_w_k, kv_w_v, peer_w_q, peer_subkeys, peer_u, peer_v, ln_g, ln_b)
    y_sample, new_conv_sample, new_k_sample, new_v_sample = _trunk(
        x_sample, state_conv, cache_k, cache_v, cache_k.shape[1], conv_w_in, conv_w_dw, conv_w_out,
        sb_w_q, sb_w_o, kv_w_k, kv_w_v, peer_w_q, peer_subkeys, peer_u, peer_v, ln_g, ln_b)
    return (y_prompt, y_sample, new_conv_prompt, new_k_prompt, new_v_prompt,
            new_conv_sample, new_k_sample, new_v_sample)
```

```python
import functools
import math

import jax
import jax.numpy as jnp
from jax import lax
from jax.experimental import pallas as pl
from jax.experimental.pallas import tpu as pltpu

F32 = jnp.float32
BF16 = jnp.bfloat16
I32 = jnp.int32

LN_EPS = 1e-5
SB_HEAD_DIM = 64
PEER_HEADS = 8
PEER_N_KEYS = 128
PEER_TOPK = 16
PEER_HALF = 128
LANES = 128
G_PITCH = 136
VMEM_LIMIT = 56 * 1024 * 1024


def _cparams(sem):
    return pltpu.CompilerParams(dimension_semantics=sem, vmem_limit_bytes=VMEM_LIMIT)


def _layernorm(v, g, b):
    mu = jnp.mean(v, axis=-1, keepdims=True)
    d = v - mu
    var = jnp.mean(d * d, axis=-1, keepdims=True)
    return d * lax.rsqrt(var + LN_EPS) * g + b


def _conv_kernel(x_ref, prev_ref, win_ref, wdw_ref, wout_ref, g_ref, b_ref,
                 o_ref, st_ref, carry_ref, *, alpha, n_seq_tiles):
    s = pl.program_id(1)
    x = x_ref[0]
    tm, d = x.shape
    proj = jnp.dot(x.astype(BF16), win_ref[...], preferred_element_type=F32)
    b_gate = proj[:, :d]
    u = proj[:, d:2 * d] * proj[:, 2 * d:]

    @pl.when(s == 0)
    def _():
        carry_ref[0:2, :] = prev_ref[0]

    p0 = carry_ref[0:1, :]
    p1 = carry_ref[1:2, :]
    row = lax.broadcasted_iota(I32, (tm, d), 0)
    um1 = jnp.where(row == 0, p1, pltpu.roll(u, 1, 0))
    um2 = jnp.where(row == 0, p0, jnp.where(row == 1, p1, pltpu.roll(u, 2, 0)))
    acc = wdw_ref[0:1, :] * um2 + wdw_ref[1:2, :] * um1 + wdw_ref[2:3, :] * u
    carry_ref[0:2, :] = u[tm - 2:tm, :]
    y = jnp.dot((b_gate * acc).astype(BF16), wout_ref[...], preferred_element_type=F32)
    o_ref[0] = _layernorm(alpha * x + y, g_ref[...], b_ref[...])

    @pl.when(s == n_seq_tiles - 1)
    def _():
        st_ref[0] = u[tm - 2:tm, :]


def _conv_layer(h, prev, w_in, w_dw, w_out, g, b, alpha):
    bsz, seq, d = h.shape
    tm = min(seq, 512)
    ns = seq // tm
    return pl.pallas_call(
        functools.partial(_conv_kernel, alpha=alpha, n_seq_tiles=ns),
        grid=(bsz, ns),
        in_specs=[
            pl.BlockSpec((1, tm, d), lambda i, j: (i, j, 0)),
            pl.BlockSpec((1, 2, d), lambda i, j: (i, 0, 0)),
            pl.BlockSpec((d, 3 * d), lambda i, j: (0, 0)),
            pl.BlockSpec((3, d), lambda i, j: (0, 0)),
            pl.BlockSpec((d, d), lambda i, j: (0, 0)),
            pl.BlockSpec((1, d), lambda i, j: (0, 0)),
            pl.BlockSpec((1, d), lambda i, j: (0, 0)),
        ],
        out_specs=[
            pl.BlockSpec((1, tm, d), lambda i, j: (i, j, 0)),
            pl.BlockSpec((1, 2, d), lambda i, j: (i, 0, 0)),
        ],
        out_shape=[
            jax.ShapeDtypeStruct((bsz, seq, d), F32),
            jax.ShapeDtypeStruct((bsz, 2, d), F32),
        ],
        scratch_shapes=[pltpu.VMEM((8, d), F32)],
        compiler_params=_cparams(("arbitrary", "arbitrary")),
    )(h, prev, w_in, w_dw, w_out, g, b)


def _kv_kernel(x_ref, w_ref, k_ref, v_ref, kb_ref, vb_ref):
    d = x_ref.shape[1]
    y = jnp.dot(x_ref[...].astype(BF16), w_ref[...], preferred_element_type=F32)
    k = y[:, :d]
    v = y[:, d:]
    k_ref[...] = k
    v_ref[...] = v
    kb_ref[...] = k.astype(BF16)
    vb_ref[...] = v.astype(BF16)


def _kv_proj(h2, w_kv):
    t, d = h2.shape
    tm = min(t, 512)
    blk = pl.BlockSpec((tm, d), lambda i: (i, 0))
    return pl.pallas_call(
        _kv_kernel,
        grid=(t // tm,),
        in_specs=[blk, pl.BlockSpec((d, 2 * d), lambda i: (0, 0))],
        out_specs=[blk, blk, blk, blk],
        out_shape=[
            jax.ShapeDtypeStruct((t, d), F32), jax.ShapeDtypeStruct((t, d), F32),
            jax.ShapeDtypeStruct((t, d), BF16), jax.ShapeDtypeStruct((t, d), BF16),
        ],
        compiler_params=_cparams(("arbitrary",)),
    )(h2, w_kv)


def _q_kernel(x_ref, w_ref, o_ref):
    o_ref[...] = jnp.dot(x_ref[...].astype(BF16), w_ref[...],
                         preferred_element_type=F32).astype(o_ref.dtype)


def _q_proj(h2, w_q):
    t, d = h2.shape
    tm = min(t, 512)
    blk = pl.BlockSpec((tm, d), lambda i: (i, 0))
    return pl.pallas_call(
        _q_kernel,
        grid=(t // tm,),
        in_specs=[blk, pl.BlockSpec((d, d), lambda i: (0, 0))],
        out_specs=blk,
        out_shape=jax.ShapeDtypeStruct((t, d), BF16),
        compiler_params=_cparams(("arbitrary",)),
    )(h2, w_q)


def _oproj_ln_kernel(a_ref, h_ref, w_ref, g_ref, b_ref, o_ref, *, alpha):
    y = jnp.dot(a_ref[...], w_ref[...], preferred_element_type=F32)
    o_ref[...] = _layernorm(alpha * h_ref[...] + y, g_ref[...], b_ref[...])


def _oproj_ln(a2, h2, w_o, g, b, alpha):
    t, d = h2.shape
    tm = min(t, 512)
    blk = pl.BlockSpec((tm, d), lambda i: (i, 0))
    vec = pl.BlockSpec((1, d), lambda i: (0, 0))
    return pl.pallas_call(
        functools.partial(_oproj_ln_kernel, alpha=alpha),
        grid=(t // tm,),
        in_specs=[blk, blk, pl.BlockSpec((d, d), lambda i: (0, 0)), vec, vec],
        out_specs=blk,
        out_shape=jax.ShapeDtypeStruct((t, d), F32),
        compiler_params=_cparams(("arbitrary",)),
    )(a2, h2, w_o, g, b)


def _sb_kernel(q_ref, k_ref, v_ref, tri_ref, o_ref, *, q_pos0, tq, tk, n_kblocks):
    qi = pl.program_id(1)
    q2 = q_ref[0]
    lane = lax.broadcasted_iota(I32, (1, LANES), 1)
    first = lane < SB_HEAD_DIM
    zero = jnp.zeros_like(q2)
    q_heads = (jnp.where(first, q2, zero), jnp.where(first, zero, q2))
    q_start = q_pos0 + qi * tq
    qpos = q_start + lax.broadcasted_iota(I32, (tq, 1), 0)
    scale = SB_HEAD_DIM ** -0.5
    nblk = jnp.minimum(n_kblocks, (q_start + tq - 1 + tk - 1) // tk)

    def body(j, carry):
        kb = nblk - 1 - j
        start = pl.multiple_of(kb * tk, tk)
        kblk = k_ref[0, pl.ds(start, tk), :]
        vblk = v_ref[0, pl.ds(start, tk), :]
        kpos = start + lax.broadcasted_iota(I32, (1, tk), 1)
        causal = kpos < qpos
        out = []
        for hd in range(2):
            acc, run = carry[2 * hd], carry[2 * hd + 1]
            z = lax.dot_general(q_heads[hd], kblk, (((1,), (1,)), ((), ())),
                                preferred_element_type=F32) * scale
            sp = jnp.maximum(z, 0.0) + jnp.log(1.0 + jnp.exp(-jnp.abs(z)))
            log_not = jnp.where(causal, -sp, 0.0)
            suffix = jnp.dot(log_not.astype(BF16), tri_ref[...],
                             preferred_element_type=F32) + run
            a = jnp.where(causal, jnp.exp((z - sp) + suffix), 0.0)
            acc = acc + jnp.dot(a.astype(BF16), vblk, preferred_element_type=F32)
            run = run + jnp.sum(log_not, axis=1, keepdims=True)
            out += [acc, run]
        return tuple(out)

    init = (jnp.zeros((tq, LANES), F32), jnp.zeros((tq, 1), F32),
            jnp.zeros((tq, LANES), F32), jnp.zeros((tq, 1), F32))
    res = lax.fori_loop(0, nblk, body, init)
    o_ref[0] = jnp.where(first, res[0], res[2]).astype(o_ref.dtype)


def _sb_attention(q, k_all, v_all, q_pos0):
    bsz, sq, d = q.shape
    tq = min(sq, 256)
    tk = 256
    sk = k_all.shape[1]
    n_kblocks = sk // tk
    tri = (lax.broadcasted_iota(I32, (tk, tk), 0) >
           lax.broadcasted_iota(I32, (tk, tk), 1)).astype(BF16)
    return pl.pallas_call(
        functools.partial(_sb_kernel, q_pos0=q_pos0, tq=tq, tk=tk, n_kblocks=n_kblocks),
        grid=(bsz, sq // tq, d // LANES),
        in_specs=[
            pl.BlockSpec((1, tq, LANES), lambda b, i, p: (b, i, p)),
            pl.BlockSpec((1, sk, LANES), lambda b, i, p: (b, 0, p)),
            pl.BlockSpec((1, sk, LANES), lambda b, i, p: (b, 0, p)),
            pl.BlockSpec((tk, tk), lambda b, i, p: (0, 0)),
        ],
        out_specs=pl.BlockSpec((1, tq, LANES), lambda b, i, p: (b, i, p)),
        out_shape=jax.ShapeDtypeStruct((bsz, sq, d), BF16),
        compiler_params=_cparams(("arbitrary", "arbitrary", "arbitrary")),
    )(q, k_all, v_all, tri)


def _top16_rows(s):
    row = lax.broadcasted_iota(I32, s.shape, 0)
    vals, idxs = [], []
    for _ in range(PEER_TOPK):
        m = jnp.max(s, axis=0, keepdims=True)
        sel = jnp.min(jnp.where(s == m, row, PEER_N_KEYS), axis=0, keepdims=True)
        vals.append(m)
        idxs.append(sel)
        s = jnp.where(row == sel, -jnp.inf, s)
    return jnp.concatenate(vals, axis=0), jnp.concatenate(idxs, axis=0)


def _cand_rows(x1, x2):
    k = PEER_TOPK
    n = x1.shape[1]
    parts = [(jnp.broadcast_to(x1[0:1], (k, n)), x2)]
    for a in range(1, 8):
        parts.append((jnp.broadcast_to(x1[a:a + 1], (8, n)), x2[0:8]))
    parts.append((x1[8:16], jnp.broadcast_to(x2[0:1], (8, n))))
    return (jnp.concatenate([p[0] for p in parts], axis=0),
            jnp.concatenate([p[1] for p in parts], axis=0))


def _cand_order():
    k = PEER_TOPK
    pairs = [(0, b) for b in range(k)]
    for a in range(1, 8):
        pairs += [(a, b) for b in range(8)]
    pairs += [(a, 0) for a in range(8, k)]
    flat = [a * k + b if (a + 1) * (b + 1) <= k else 1 << 20 for a, b in pairs]
    return jnp.broadcast_to(jnp.asarray(flat, I32)[:, None], (len(flat), LANES))


def _route_kernel(h_ref, wq_ref, sk_ref, order_ref, i_ref, j_ref, w_ref,
                  q_s, v_s, n_s, it_s, jt_s, wt_s):
    tr = h_ref.shape[0]
    nsub = tr // LANES
    x = h_ref[...].astype(BF16)
    q_t = lax.dot_general(wq_ref[...], x, (((1,), (1,)), ((), ())),
                          preferred_element_type=F32)
    q_s[...] = q_t.astype(BF16)

    def half_body(hp, _):
        qs = q_s[pl.ds(pl.multiple_of(hp * PEER_HALF, PEER_HALF), PEER_HALF), :]
        s_t = jnp.dot(sk_ref[hp], qs, preferred_element_type=F32)
        for sub in range(nsub):
            sl = slice(sub * LANES, (sub + 1) * LANES)
            v, n = _top16_rows(s_t[:, sl])
            v_s[hp, :, sl] = v
            n_s[hp, :, sl] = n
        return 0

    lax.fori_loop(0, 2 * PEER_HEADS, half_body, 0)

    order = order_ref[...]
    valid = order < (1 << 20)

    def head_body(hd, _):
        for sub in range(nsub):
            sl = slice(sub * LANES, (sub + 1) * LANES)
            v1, v2 = v_s[2 * hd, :, sl], v_s[2 * hd + 1, :, sl]
            n1, n2 = n_s[2 * hd, :, sl], n_s[2 * hd + 1, :, sl]
            c1, c2 = _cand_rows(v1, v2)
            cand = jnp.where(valid, c1 + c2, -jnp.inf)
            ci, cj = _cand_rows(n1, n2)
            best, bi, bj = [], [], []
            for _ in range(PEER_TOPK):
                m = jnp.max(cand, axis=0, keepdims=True)
                sel = jnp.min(jnp.where(cand == m, order, 1 << 21), axis=0, keepdims=True)
                hit = order == sel
                best.append(m)
                bi.append(jnp.max(jnp.where(hit, ci, -1), axis=0, keepdims=True))
                bj.append(jnp.max(jnp.where(hit, cj, -1), axis=0, keepdims=True))
                cand = jnp.where(hit, -jnp.inf, cand)
            best = jnp.concatenate(best, axis=0)
            e = jnp.exp(best - best[0:1])
            gate = e / jnp.sum(e, axis=0, keepdims=True)
            rows = pl.ds(pl.multiple_of(hd * PEER_TOPK, PEER_TOPK), PEER_TOPK)
            it_s[rows, sl] = jnp.concatenate(bi, axis=0)
            jt_s[rows, sl] = jnp.concatenate(bj, axis=0)
            wt_s[rows, sl] = gate
        return 0

    lax.fori_loop(0, PEER_HEADS, head_body, 0)
    i_ref[...] = it_s[...].T
    j_ref[...] = jt_s[...].T
    w_ref[...] = wt_s[...].T


def _peer_route(h2, wq_t, sk, order):
    t, d = h2.shape
    tr = min(t, 256)
    assert t % tr == 0 and tr % LANES == 0, (t, tr)
    nq = wq_t.shape[0]
    hk = PEER_HEADS * PEER_TOPK
    blk = pl.BlockSpec((tr, hk), lambda i: (i, 0))
    return pl.pallas_call(
        _route_kernel,
        grid=(t // tr,),
        in_specs=[
            pl.BlockSpec((tr, d), lambda i: (i, 0)),
            pl.BlockSpec((nq, d), lambda i: (0, 0)),
            pl.BlockSpec(sk.shape, lambda i: (0, 0, 0)),
            pl.BlockSpec(order.shape, lambda i: (0, 0)),
        ],
        out_specs=[blk, blk, blk],
        out_shape=[jax.ShapeDtypeStruct((t, hk), I32), jax.ShapeDtypeStruct((t, hk), I32),
                   jax.ShapeDtypeStruct((t, hk), F32)],
        scratch_shapes=[
            pltpu.VMEM((nq, tr), BF16),
            pltpu.VMEM((2 * PEER_HEADS, PEER_TOPK, tr), F32),
            pltpu.VMEM((2 * PEER_HEADS, PEER_TOPK, tr), I32),
            pltpu.VMEM((hk, tr), I32),
            pltpu.VMEM((hk, tr), I32),
            pltpu.VMEM((hk, tr), F32),
        ],
        compiler_params=_cparams(("arbitrary",)),
    )(h2, wq_t, sk, order)


def _expert_kernel(h_ref, i_ref, j_ref, w_ref, ut_ref, v_ref, g_ref, b_ref, o_ref,
                   xb_s, gate_s, acc_s, *, alpha, n_chunks, sub_e):
    c = pl.program_id(1)
    tm = h_ref.shape[0]
    ec = ut_ref.shape[1]
    hk = i_ref.shape[1]

    @pl.when(c == 0)
    def _():
        xb_s[...] = h_ref[...].astype(BF16)
        acc_s[...] = jnp.zeros_like(acc_s)
        key = lax.broadcasted_iota(I32, (PEER_N_KEYS, hk), 0)
        zeros = jnp.zeros((PEER_N_KEYS, hk), BF16)

        def pair_body(p, _):
            lhs, rhs = [], []
            for r in range(2):
                t = 2 * p + r
                it = jnp.broadcast_to(i_ref[pl.ds(t, 1), :], (PEER_N_KEYS, hk))
                jt = jnp.broadcast_to(j_ref[pl.ds(t, 1), :], (PEER_N_KEYS, hk))
                wt = jnp.broadcast_to(w_ref[pl.ds(t, 1), :], (PEER_N_KEYS, hk))
                lhs.append(jnp.where(key == it, 1.0, 0.0).astype(BF16))
                rhs.append(jnp.where(key == jt, wt, 0.0).astype(BF16))
            x_mat = jnp.concatenate(lhs, axis=1)
            y_mat = jnp.concatenate(
                [jnp.concatenate([rhs[0], zeros], axis=1),
                 jnp.concatenate([zeros, rhs[1]], axis=1)], axis=0)
            gp = lax.dot_general(x_mat, y_mat, (((1,), (1,)), ((), ())),
                                 preferred_element_type=F32)
            for r in range(2):
                base = pl.multiple_of((2 * p + r) * G_PITCH, 8)
                gate_s[pl.ds(base, PEER_N_KEYS), :] = gp[:, r * LANES:(r + 1) * LANES]
            return 0

        lax.fori_loop(0, tm // 2, pair_body, 0)

    i_base = c * (ec // PEER_N_KEYS)
    per_sub = sub_e // PEER_N_KEYS
    for sc in range(ec // sub_e):
        a = jnp.dot(xb_s[...], ut_ref[:, sc * sub_e:(sc + 1) * sub_e],
                    preferred_element_type=F32)
        gate = jnp.concatenate(
            [gate_s[pl.ds(i_base + sc * per_sub + r, tm, stride=G_PITCH), :]
             for r in range(per_sub)], axis=1)
        act = 0.5 * a * (1.0 + lax.erf(a * (2.0 ** -0.5)))
        hm = (act * gate).astype(BF16)
        acc_s[...] += jnp.dot(hm, v_ref[sc * sub_e:(sc + 1) * sub_e, :],
                              preferred_element_type=F32)

    @pl.when(c == n_chunks - 1)
    def _():
        o_ref[...] = _layernorm(alpha * h_ref[...] + acc_s[...], g_ref[...], b_ref[...])


def _peer_experts(h2, ri, rj, rw, u_t, v_tab, g, b, alpha):
    t, d = h2.shape
    n_exp = v_tab.shape[0]
    tm = min(t, 256)
    ec = 2048
    assert t % tm == 0 and tm % 2 == 0 and n_exp % ec == 0, (t, n_exp)
    nc = n_exp // ec
    hk = ri.shape[1]
    tok = pl.BlockSpec((tm, d), lambda i, c: (i, 0))
    rt = pl.BlockSpec((tm, hk), lambda i, c: (i, 0))
    vec = pl.BlockSpec((1, d), lambda i, c: (0, 0))
    return pl.pallas_call(
        functools.partial(_expert_kernel, alpha=alpha, n_chunks=nc, sub_e=512),
        grid=(t // tm, nc),
        in_specs=[tok, rt, rt, rt,
                  pl.BlockSpec((d, ec), lambda i, c: (0, c)),
                  pl.BlockSpec((ec, d), lambda i, c: (c, 0)),
                  vec, vec],
        out_specs=tok,
        out_shape=jax.ShapeDtypeStruct((t, d), F32),
        scratch_shapes=[
            pltpu.VMEM((tm, d), BF16),
            pltpu.VMEM((tm * G_PITCH, LANES), F32),
            pltpu.VMEM((tm, d), F32),
        ],
        compiler_params=_cparams(("arbitrary", "arbitrary")),
    )(h2, ri, rj, rw, u_t, v_tab, g, b)


def _trunk(x, conv_prev, k_past, v_past, q_pos0, wts):
    bsz, seq, d = x.shape
    t = bsz * seq
    n_a = wts["conv_w_in"].shape[0]
    depth = n_a + wts["sb_w_q"].shape[0]
    alpha = (2.0 * depth) ** 0.25
    h = x
    new_conv = []
    k_new = v_new = k_all = v_all = None
    for layer in range(depth):
        g0, b0 = wts["ln_g"][layer, 0][None], wts["ln_b"][layer, 0][None]
        g1, b1 = wts["ln_g"][layer, 1][None], wts["ln_b"][layer, 1][None]
        if layer < n_a:
            h, st = _conv_layer(h, conv_prev[layer], wts["conv_w_in"][layer],
                                wts["conv_w_dw"][layer], wts["conv_w_out"][layer], g0, b0, alpha)
            new_conv.append(st)
        else:
            h2 = h.reshape(t, d)
            if layer == n_a:
                k2, v2, kb, vb = _kv_proj(h2, wts["kv_w"])
                k_new = k2.reshape(bsz, seq, d // SB_HEAD_DIM, SB_HEAD_DIM)
                v_new = v2.reshape(bsz, seq, d // SB_HEAD_DIM, SB_HEAD_DIM)
                k_all, v_all = kb.reshape(bsz, seq, d), vb.reshape(bsz, seq, d)
                if k_past is not None:
                    past = k_past.shape[1]
                    k_all = jnp.concatenate([k_past.reshape(bsz, past, d).astype(BF16), k_all], axis=1)
                    v_all = jnp.concatenate([v_past.reshape(bsz, past, d).astype(BF16), v_all], axis=1)
                pad = (-k_all.shape[1]) % 256
                if pad:
                    k_all = jnp.pad(k_all, ((0, 0), (0, pad), (0, 0)))
                    v_all = jnp.pad(v_all, ((0, 0), (0, pad), (0, 0)))
            j = layer - n_a
            q = _q_proj(h2, wts["sb_w_q"][j]).reshape(bsz, seq, d)
            att = _sb_attention(q, k_all, v_all, q_pos0)
            h = _oproj_ln(att.reshape(t, d), h2, wts["sb_w_o"][j], g0, b0, alpha).reshape(bsz, seq, d)
        h2 = h.reshape(t, d)
        ri, rj, rw = _peer_route(h2, wts["peer_wq_t"][layer], wts["peer_sk"][layer], wts["order"])
        h = _peer_experts(h2, ri, rj, rw, wts["peer_u_t"][layer], wts["peer_v"][layer],
                          g1, b1, alpha).reshape(bsz, seq, d)
    return h, jnp.stack(new_conv, axis=0), k_new, v_new


def kernel(x_prompt, x_sample, state_conv, cache_k, cache_v, conv_w_in, conv_w_dw, conv_w_out,
           sb_w_q, sb_w_o, kv_w_k, kv_w_v, peer_w_q, peer_subkeys, peer_u, peer_v, ln_g, ln_b):
    depth = peer_w_q.shape[0]
    wts = {
        "conv_w_in": conv_w_in.astype(BF16),
        "conv_w_dw": conv_w_dw,
        "conv_w_out": conv_w_out.astype(BF16),
        "sb_w_q": sb_w_q.astype(BF16),
        "sb_w_o": sb_w_o.astype(BF16),
        "kv_w": jnp.concatenate([kv_w_k, kv_w_v], axis=1).astype(BF16),
        "peer_wq_t": jnp.swapaxes(peer_w_q, 1, 2).astype(BF16),
        "peer_sk": peer_subkeys.reshape(depth, 2 * PEER_HEADS, PEER_N_KEYS, PEER_HALF).astype(BF16),
        "peer_u_t": jnp.swapaxes(peer_u, 1, 2).astype(BF16),
        "peer_v": peer_v.astype(BF16),
        "ln_g": ln_g,
        "ln_b": ln_b,
        "order": _cand_order(),
    }
    n_a = conv_w_in.shape[0]
    zero_conv = jnp.zeros((n_a, x_prompt.shape[0], state_conv.shape[2], x_prompt.shape[2]), x_prompt.dtype)
    y_p, conv_p, k_p, v_p = _trunk(x_prompt, zero_conv, None, None, 0, wts)
    y_s, conv_s, k_s, v_s = _trunk(x_sample, state_conv, cache_k, cache_v, cache_k.shape[1], wts)
    return (y_p, y_s, conv_p, k_p, v_p, conv_s, k_s, v_s)
```

```python
import functools
import math

import jax
import jax.numpy as jnp
from jax import lax
from jax.experimental import pallas as pl
from jax.experimental.pallas import tpu as pltpu

F32 = jnp.float32
BF16 = jnp.bfloat16
I32 = jnp.int32

LN_EPS = 1e-5
SB_HEAD_DIM = 64
PEER_HEADS = 8
PEER_N_KEYS = 128
PEER_TOPK = 16
PEER_HALF = 128
LANES = 128
G_PITCH = 136
VMEM_LIMIT = 56 * 1024 * 1024
EXP_UNDERFLOW = -104.0
G_GROUP = 16


def _cparams(sem):
    return pltpu.CompilerParams(dimension_semantics=sem, vmem_limit_bytes=VMEM_LIMIT)


def _layernorm(v, g, b):
    mu = jnp.mean(v, axis=-1, keepdims=True)
    d = v - mu
    var = jnp.mean(d * d, axis=-1, keepdims=True)
    return d * lax.rsqrt(var + LN_EPS) * g + b


def _conv_kernel(x_ref, prev_ref, win_ref, wdw_ref, wout_ref, g_ref, b_ref,
                 o_ref, st_ref, carry_ref, *, alpha, n_seq_tiles):
    s = pl.program_id(1)
    x = x_ref[0]
    tm, d = x.shape
    proj = jnp.dot(x.astype(BF16), win_ref[...], preferred_element_type=F32)
    b_gate = proj[:, :d]
    u = proj[:, d:2 * d] * proj[:, 2 * d:]

    @pl.when(s == 0)
    def _():
        carry_ref[0:2, :] = prev_ref[0]

    p0 = carry_ref[0:1, :]
    p1 = carry_ref[1:2, :]
    row = lax.broadcasted_iota(I32, (tm, d), 0)
    um1 = jnp.where(row == 0, p1, pltpu.roll(u, 1, 0))
    um2 = jnp.where(row == 0, p0, jnp.where(row == 1, p1, pltpu.roll(u, 2, 0)))
    acc = wdw_ref[0:1, :] * um2 + wdw_ref[1:2, :] * um1 + wdw_ref[2:3, :] * u
    carry_ref[0:2, :] = u[tm - 2:tm, :]
    y = jnp.dot((b_gate * acc).astype(BF16), wout_ref[...], preferred_element_type=F32)
    o_ref[0] = _layernorm(alpha * x + y, g_ref[...], b_ref[...])

    @pl.when(s == n_seq_tiles - 1)
    def _():
        st_ref[0] = u[tm - 2:tm, :]


def _conv_layer(h, prev, w_in, w_dw, w_out, g, b, alpha):
    bsz, seq, d = h.shape
    tm = min(seq, 512)
    ns = seq // tm
    return pl.pallas_call(
        functools.partial(_conv_kernel, alpha=alpha, n_seq_tiles=ns),
        grid=(bsz, ns),
        in_specs=[
            pl.BlockSpec((1, tm, d), lambda i, j: (i, j, 0)),
            pl.BlockSpec((1, 2, d), lambda i, j: (i, 0, 0)),
            pl.BlockSpec((d, 3 * d), lambda i, j: (0, 0)),
            pl.BlockSpec((3, d), lambda i, j: (0, 0)),
            pl.BlockSpec((d, d), lambda i, j: (0, 0)),
            pl.BlockSpec((1, d), lambda i, j: (0, 0)),
            pl.BlockSpec((1, d), lambda i, j: (0, 0)),
        ],
        out_specs=[
            pl.BlockSpec((1, tm, d), lambda i, j: (i, j, 0)),
            pl.BlockSpec((1, 2, d), lambda i, j: (i, 0, 0)),
        ],
        out_shape=[
            jax.ShapeDtypeStruct((bsz, seq, d), F32),
            jax.ShapeDtypeStruct((bsz, 2, d), F32),
        ],
        scratch_shapes=[pltpu.VMEM((8, d), F32)],
        name="conv_mixer",
        compiler_params=_cparams(("arbitrary", "arbitrary")),
    )(h, prev, w_in, w_dw, w_out, g, b)


def _kv_kernel(x_ref, w_ref, k_ref, v_ref, kb_ref, vb_ref):
    d = x_ref.shape[1]
    y = jnp.dot(x_ref[...].astype(BF16), w_ref[...], preferred_element_type=F32)
    k = y[:, :d]
    v = y[:, d:]
    k_ref[...] = k
    v_ref[...] = v
    kb_ref[...] = k.astype(BF16)
    vb_ref[...] = v.astype(BF16)


def _kv_proj(h2, w_kv):
    t, d = h2.shape
    tm = min(t, 512)
    blk = pl.BlockSpec((tm, d), lambda i: (i, 0))
    return pl.pallas_call(
        _kv_kernel,
        grid=(t // tm,),
        in_specs=[blk, pl.BlockSpec((d, 2 * d), lambda i: (0, 0))],
        out_specs=[blk, blk, blk, blk],
        out_shape=[
            jax.ShapeDtypeStruct((t, d), F32), jax.ShapeDtypeStruct((t, d), F32),
            jax.ShapeDtypeStruct((t, d), BF16), jax.ShapeDtypeStruct((t, d), BF16),
        ],
        name="kv_proj",
        compiler_params=_cparams(("arbitrary",)),
    )(h2, w_kv)


def _q_kernel(x_ref, w_ref, o_ref, *, scale):
    o_ref[...] = (jnp.dot(x_ref[...].astype(BF16), w_ref[...],
                          preferred_element_type=F32) * scale).astype(o_ref.dtype)


def _q_proj(h2, w_q):
    t, d = h2.shape
    tm = min(t, 512)
    blk = pl.BlockSpec((tm, d), lambda i: (i, 0))
    scale = SB_HEAD_DIM ** -0.5
    assert math.frexp(scale)[0] == 0.5, scale
    return pl.pallas_call(
        functools.partial(_q_kernel, scale=scale),
        grid=(t // tm,),
        in_specs=[blk, pl.BlockSpec((d, d), lambda i: (0, 0))],
        out_specs=blk,
        out_shape=jax.ShapeDtypeStruct((t, d), BF16),
        name="q_proj",
        compiler_params=_cparams(("arbitrary",)),
    )(h2, w_q)


def _oproj_ln_kernel(a_ref, h_ref, w_ref, g_ref, b_ref, o_ref, *, alpha):
    y = jnp.dot(a_ref[...], w_ref[...], preferred_element_type=F32)
    o_ref[...] = _layernorm(alpha * h_ref[...] + y, g_ref[...], b_ref[...])


def _oproj_ln(a2, h2, w_o, g, b, alpha):
    t, d = h2.shape
    tm = min(t, 512)
    blk = pl.BlockSpec((tm, d), lambda i: (i, 0))
    vec = pl.BlockSpec((1, d), lambda i: (0, 0))
    return pl.pallas_call(
        functools.partial(_oproj_ln_kernel, alpha=alpha),
        grid=(t // tm,),
        in_specs=[blk, blk, pl.BlockSpec((d, d), lambda i: (0, 0)), vec, vec],
        out_specs=blk,
        out_shape=jax.ShapeDtypeStruct((t, d), F32),
        name="oproj_ln",
        compiler_params=_cparams(("arbitrary",)),
    )(a2, h2, w_o, g, b)


def _sb_kernel(q_ref, k_ref, v_ref, tri_ref, o_ref, *, q_pos0, tq, tk, n_kblocks):
    qi = pl.program_id(1)
    q2 = q_ref[0]
    lane = lax.broadcasted_iota(I32, (1, LANES), 1)
    first = lane < SB_HEAD_DIM
    zero = jnp.zeros_like(q2)
    q_heads = (jnp.where(first, q2, zero), jnp.where(first, zero, q2))
    q_start = q_pos0 + qi * tq
    qpos = q_start + lax.broadcasted_iota(I32, (tq, 1), 0)
    nblk = jnp.minimum(n_kblocks, (q_start + tq - 1 + tk - 1) // tk)

    def cond(carry):
        return jnp.logical_and(carry[0] < nblk, carry[1] > 0)

    def body(carry):
        j = carry[0]
        kb = nblk - 1 - j
        start = pl.multiple_of(kb * tk, tk)
        kblk = k_ref[0, pl.ds(start, tk), :]
        vblk = v_ref[0, pl.ds(start, tk), :]
        kpos = start + lax.broadcasted_iota(I32, (1, tk), 1)
        causal = kpos < qpos
        out = []
        for hd in range(2):
            acc, run = carry[2 + 2 * hd], carry[3 + 2 * hd]
            z = lax.dot_general(q_heads[hd], kblk, (((1,), (1,)), ((), ())),
                                preferred_element_type=F32)
            log_sig = jnp.minimum(z, 0.0) - jnp.log(1.0 + jnp.exp(-jnp.abs(z)))
            log_not = jnp.where(causal, log_sig - z, 0.0)
            suffix = jnp.dot(log_not.astype(BF16), tri_ref[...],
                             preferred_element_type=F32) + run
            a = jnp.where(causal, jnp.exp(log_sig + suffix), 0.0)
            acc = acc + jnp.dot(a.astype(BF16), vblk, preferred_element_type=F32)
            run = run + jnp.sum(log_not, axis=1, keepdims=True)
            out += [acc, run]
        live = (jnp.max(jnp.maximum(out[1], out[3])) > EXP_UNDERFLOW).astype(I32)
        return (j + 1, live, *out)

    init = (jnp.int32(0), jnp.int32(1),
            jnp.zeros((tq, LANES), F32), jnp.zeros((tq, 1), F32),
            jnp.zeros((tq, LANES), F32), jnp.zeros((tq, 1), F32))
    res = lax.while_loop(cond, body, init)
    o_ref[0] = jnp.where(first, res[2], res[4]).astype(o_ref.dtype)


def _sb_attention(q, k_all, v_all, q_pos0):
    bsz, sq, d = q.shape
    tq = min(sq, 256)
    tk = 256
    sk = k_all.shape[1]
    n_kblocks = sk // tk
    tri = (lax.broadcasted_iota(I32, (tk, tk), 0) >
           lax.broadcasted_iota(I32, (tk, tk), 1)).astype(BF16)
    return pl.pallas_call(
        functools.partial(_sb_kernel, q_pos0=q_pos0, tq=tq, tk=tk, n_kblocks=n_kblocks),
        grid=(bsz, sq // tq, d // LANES),
        in_specs=[
            pl.BlockSpec((1, tq, LANES), lambda b, i, p: (b, i, p)),
            pl.BlockSpec((1, sk, LANES), lambda b, i, p: (b, 0, p)),
            pl.BlockSpec((1, sk, LANES), lambda b, i, p: (b, 0, p)),
            pl.BlockSpec((tk, tk), lambda b, i, p: (0, 0)),
        ],
        out_specs=pl.BlockSpec((1, tq, LANES), lambda b, i, p: (b, i, p)),
        out_shape=jax.ShapeDtypeStruct((bsz, sq, d), BF16),
        name="sb_attention",
        compiler_params=_cparams(("arbitrary", "arbitrary", "arbitrary")),
    )(q, k_all, v_all, tri)


def _top16_rows(s):
    row = lax.broadcasted_iota(I32, s.shape, 0)
    vals, idxs = [], []
    for _ in range(PEER_TOPK):
        m = jnp.max(s, axis=0, keepdims=True)
        sel = jnp.min(jnp.where(s == m, row, PEER_N_KEYS), axis=0, keepdims=True)
        vals.append(m)
        idxs.append(sel)
        s = jnp.where(row == sel, -jnp.inf, s)
    return jnp.concatenate(vals, axis=0), jnp.concatenate(idxs, axis=0)


def _cand_rows(x1, x2):
    k = PEER_TOPK
    n = x1.shape[1]
    parts = [(jnp.broadcast_to(x1[0:1], (k, n)), x2)]
    for a in range(1, 8):
        parts.append((jnp.broadcast_to(x1[a:a + 1], (8, n)), x2[0:8]))
    parts.append((x1[8:16], jnp.broadcast_to(x2[0:1], (8, n))))
    return (jnp.concatenate([p[0] for p in parts], axis=0),
            jnp.concatenate([p[1] for p in parts], axis=0))


def _cand_order():
    k = PEER_TOPK
    pairs = [(0, b) for b in range(k)]
    for a in range(1, 8):
        pairs += [(a, b) for b in range(8)]
    pairs += [(a, 0) for a in range(8, k)]
    flat = [a * k + b if (a + 1) * (b + 1) <= k else 1 << 20 for a, b in pairs]
    return jnp.broadcast_to(jnp.asarray(flat, I32)[:, None], (len(flat), LANES))


def _product_top16(v1, n1, v2, n2, order, valid):
    c1, c2 = _cand_rows(v1, v2)
    cand = jnp.where(valid, c1 + c2, -jnp.inf)
    best, code = [], []
    for _ in range(PEER_TOPK):
        m = jnp.max(cand, axis=0, keepdims=True)
        sel = jnp.min(jnp.where(cand == m, order, 1 << 21), axis=0, keepdims=True)
        best.append(m)
        code.append(sel)
        cand = jnp.where(order == sel, -jnp.inf, cand)
    code = jnp.concatenate(code, axis=0)
    rank_a, rank_b = code >> 4, code & (PEER_TOPK - 1)
    bi = jnp.zeros_like(code)
    bj = jnp.zeros_like(code)
    for r in range(PEER_TOPK):
        bi = jnp.where(rank_a == r, n1[r:r + 1], bi)
        bj = jnp.where(rank_b == r, n2[r:r + 1], bj)
    return jnp.concatenate(best, axis=0), bi, bj


def _route_kernel(h_ref, wq_ref, sk_ref, order_ref, i_ref, j_ref, w_ref,
                  q_s, v_s, n_s, it_s, jt_s, wt_s):
    tr = h_ref.shape[0]
    nsub = tr // LANES
    x = h_ref[...].astype(BF16)
    q_t = lax.dot_general(wq_ref[...], x, (((1,), (1,)), ((), ())),
                          preferred_element_type=F32)
    q_s[...] = q_t.astype(BF16)

    def half_body(hp, _):
        qs = q_s[pl.ds(pl.multiple_of(hp * PEER_HALF, PEER_HALF), PEER_HALF), :]
        s_t = jnp.dot(sk_ref[hp], qs, preferred_element_type=F32)
        for sub in range(nsub):
            sl = slice(sub * LANES, (sub + 1) * LANES)
            v, n = _top16_rows(s_t[:, sl])
            v_s[hp, :, sl] = v
            n_s[hp, :, sl] = n
        return 0

    lax.fori_loop(0, 2 * PEER_HEADS, half_body, 0)

    order = order_ref[...]
    valid = order < (1 << 20)

    def head_body(hd, _):
        for sub in range(nsub):
            sl = slice(sub * LANES, (sub + 1) * LANES)
            v1, v2 = v_s[2 * hd, :, sl], v_s[2 * hd + 1, :, sl]
            n1, n2 = n_s[2 * hd, :, sl], n_s[2 * hd + 1, :, sl]
            best, bi, bj = _product_top16(v1, n1, v2, n2, order, valid)
            e = jnp.exp(best - best[0:1])
            gate = e / jnp.sum(e, axis=0, keepdims=True)
            rows = pl.ds(pl.multiple_of(hd * PEER_TOPK, PEER_TOPK), PEER_TOPK)
            it_s[rows, sl] = bi
            jt_s[rows, sl] = bj
            wt_s[rows, sl] = gate
        return 0

    lax.fori_loop(0, PEER_HEADS, head_body, 0)
    i_ref[...] = it_s[...].T
    j_ref[...] = jt_s[...].T
    w_ref[...] = wt_s[...].T


def _peer_route(h2, wq_t, sk, order):
    t, d = h2.shape
    tr = min(t, 512)
    assert t % tr == 0 and tr % LANES == 0, (t, tr)
    nq = wq_t.shape[0]
    hk = PEER_HEADS * PEER_TOPK
    blk = pl.BlockSpec((tr, hk), lambda i: (i, 0))
    return pl.pallas_call(
        _route_kernel,
        grid=(t // tr,),
        in_specs=[
            pl.BlockSpec((tr, d), lambda i: (i, 0)),
            pl.BlockSpec((nq, d), lambda i: (0, 0)),
            pl.BlockSpec(sk.shape, lambda i: (0, 0, 0)),
            pl.BlockSpec(order.shape, lambda i: (0, 0)),
        ],
        out_specs=[blk, blk, blk],
        out_shape=[jax.ShapeDtypeStruct((t, hk), I32), jax.ShapeDtypeStruct((t, hk), I32),
                   jax.ShapeDtypeStruct((t, hk), F32)],
        scratch_shapes=[
            pltpu.VMEM((nq, tr), BF16),
            pltpu.VMEM((2 * PEER_HEADS, PEER_TOPK, tr), F32),
            pltpu.VMEM((2 * PEER_HEADS, PEER_TOPK, tr), I32),
            pltpu.VMEM((hk, tr), I32),
            pltpu.VMEM((hk, tr), I32),
            pltpu.VMEM((hk, tr), F32),
        ],
        name="peer_route",
        compiler_params=_cparams(("arbitrary",)),
    )(h2, wq_t, sk, order)


def _expert_kernel(h_ref, i_ref, j_ref, w_ref, ut_ref, v_ref, g_ref, b_ref, o_ref,
                   xb_s, gate_s, acc_s, *, alpha, n_chunks, sub_e):
    c = pl.program_id(1)
    tm = h_ref.shape[0]
    ec = ut_ref.shape[1]
    hk = i_ref.shape[1]

    @pl.when(c == 0)
    def _():
        xb_s[...] = h_ref[...].astype(BF16)
        acc_s[...] = jnp.zeros_like(acc_s)
        key = lax.broadcasted_iota(I32, (PEER_N_KEYS, hk), 0)
        zeros = jnp.zeros((PEER_N_KEYS, hk), BF16)

        def group_body(p, _):
            t0 = pl.multiple_of(p * G_GROUP, G_GROUP)
            i_blk = i_ref[pl.ds(t0, G_GROUP), :]
            j_blk = j_ref[pl.ds(t0, G_GROUP), :]
            w_blk = w_ref[pl.ds(t0, G_GROUP), :]
            for pair in range(G_GROUP // 2):
                lhs, rhs = [], []
                for r in (2 * pair, 2 * pair + 1):
                    it = jnp.broadcast_to(i_blk[r:r + 1, :], (PEER_N_KEYS, hk))
                    jt = jnp.broadcast_to(j_blk[r:r + 1, :], (PEER_N_KEYS, hk))
                    wt = jnp.broadcast_to(w_blk[r:r + 1, :], (PEER_N_KEYS, hk))
                    lhs.append(jnp.where(key == it, 1.0, 0.0).astype(BF16))
                    rhs.append(jnp.where(key == jt, wt, 0.0).astype(BF16))
                x_mat = jnp.concatenate(lhs, axis=1)
                y_mat = jnp.concatenate(
                    [jnp.concatenate([rhs[0], zeros], axis=1),
                     jnp.concatenate([zeros, rhs[1]], axis=1)], axis=0)
                gp = lax.dot_general(x_mat, y_mat, (((1,), (1,)), ((), ())),
                                     preferred_element_type=F32)
                for r in range(2):
                    base = pl.multiple_of((t0 + 2 * pair + r) * G_PITCH, 8)
                    gate_s[pl.ds(base, PEER_N_KEYS), :] = gp[:, r * LANES:(r + 1) * LANES]
            return 0

        lax.fori_loop(0, tm // G_GROUP, group_body, 0)

    i_base = c * (ec // PEER_N_KEYS)
    per_sub = sub_e // PEER_N_KEYS
    for sc in range(ec // sub_e):
        a = jnp.dot(xb_s[...], ut_ref[:, sc * sub_e:(sc + 1) * sub_e],
                    preferred_element_type=F32)
        gate = jnp.concatenate(
            [gate_s[pl.ds(i_base + sc * per_sub + r, tm, stride=G_PITCH), :]
             for r in range(per_sub)], axis=1)
        act = 0.5 * a * (1.0 + lax.erf(a * (2.0 ** -0.5)))
        hm = (act * gate).astype(BF16)
        acc_s[...] += jnp.dot(hm, v_ref[sc * sub_e:(sc + 1) * sub_e, :],
                              preferred_element_type=F32)

    @pl.when(c == n_chunks - 1)
    def _():
        o_ref[...] = _layernorm(alpha * h_ref[...] + acc_s[...], g_ref[...], b_ref[...])


def _peer_experts(h2, ri, rj, rw, u_t, v_tab, g, b, alpha):
    t, d = h2.shape
    n_exp = v_tab.shape[0]
    tm = min(t, 256)
    ec = 2048
    assert t % tm == 0 and tm % G_GROUP == 0 and n_exp % ec == 0, (t, n_exp)
    nc = n_exp // ec
    hk = ri.shape[1]
    tok = pl.BlockSpec((tm, d), lambda i, c: (i, 0))
    rt = pl.BlockSpec((tm, hk), lambda i, c: (i, 0))
    vec = pl.BlockSpec((1, d), lambda i, c: (0, 0))
    return pl.pallas_call(
        functools.partial(_expert_kernel, alpha=alpha, n_chunks=nc, sub_e=512),
        grid=(t // tm, nc),
        in_specs=[tok, rt, rt, rt,
                  pl.BlockSpec((d, ec), lambda i, c: (0, c)),
                  pl.BlockSpec((ec, d), lambda i, c: (c, 0)),
                  vec, vec],
        out_specs=tok,
        out_shape=jax.ShapeDtypeStruct((t, d), F32),
        scratch_shapes=[
            pltpu.VMEM((tm, d), BF16),
            pltpu.VMEM((tm * G_PITCH, LANES), F32),
            pltpu.VMEM((tm, d), F32),
        ],
        name="peer_experts",
        compiler_params=_cparams(("arbitrary", "arbitrary")),
    )(h2, ri, rj, rw, u_t, v_tab, g, b)


def _trunk(x, conv_prev, k_past, v_past, q_pos0, wts):
    bsz, seq, d = x.shape
    t = bsz * seq
    n_a = wts["conv_w_in"].shape[0]
    depth = n_a + wts["sb_w_q"].shape[0]
    alpha = (2.0 * depth) ** 0.25
    h = x
    new_conv = []
    k_new = v_new = k_all = v_all = None
    for layer in range(depth):
        g0, b0 = wts["ln_g"][layer, 0][None], wts["ln_b"][layer, 0][None]
        g1, b1 = wts["ln_g"][layer, 1][None], wts["ln_b"][layer, 1][None]
        if layer < n_a:
            h, st = _conv_layer(h, conv_prev[layer], wts["conv_w_in"][layer],
                                wts["conv_w_dw"][layer], wts["conv_w_out"][layer], g0, b0, alpha)
            new_conv.append(st)
        else:
            h2 = h.reshape(t, d)
            if layer == n_a:
                k2, v2, kb, vb = _kv_proj(h2, wts["kv_w"])
                k_new = k2.reshape(bsz, seq, d // SB_HEAD_DIM, SB_HEAD_DIM)
                v_new = v2.reshape(bsz, seq, d // SB_HEAD_DIM, SB_HEAD_DIM)
                k_all, v_all = kb.reshape(bsz, seq, d), vb.reshape(bsz, seq, d)
                if k_past is not None:
                    past = k_past.shape[1]
                    k_all = jnp.concatenate([k_past.reshape(bsz, past, d).astype(BF16), k_all], axis=1)
                    v_all = jnp.concatenate([v_past.reshape(bsz, past, d).astype(BF16), v_all], axis=1)
                pad = (-k_all.shape[1]) % 256
                if pad:
                    k_all = jnp.pad(k_all, ((0, 0), (0, pad), (0, 0)))
                    v_all = jnp.pad(v_all, ((0, 0), (0, pad), (0, 0)))
            j = layer - n_a
            q = _q_proj(h2, wts["sb_w_q"][j]).reshape(bsz, seq, d)
            att = _sb_attention(q, k_all, v_all, q_pos0)
            h = _oproj_ln(att.reshape(t, d), h2, wts["sb_w_o"][j], g0, b0, alpha).reshape(bsz, seq, d)
        h2 = h.reshape(t, d)
        ri, rj, rw = _peer_route(h2, wts["peer_wq_t"][layer], wts["peer_sk"][layer], wts["order"])
        h = _peer_experts(h2, ri, rj, rw, wts["peer_u_t"][layer], wts["peer_v"][layer],
                          g1, b1, alpha).reshape(bsz, seq, d)
    return h, jnp.stack(new_conv, axis=0), k_new, v_new


def kernel(x_prompt, x_sample, state_conv, cache_k, cache_v, conv_w_in, conv_w_dw, conv_w_out,
           sb_w_q, sb_w_o, kv_w_k, kv_w_v, peer_w_q, peer_subkeys, peer_u, peer_v, ln_g, ln_b):
    depth = peer_w_q.shape[0]
    wts = {
        "conv_w_in": conv_w_in.astype(BF16),
        "conv_w_dw": conv_w_dw,
        "conv_w_out": conv_w_out.astype(BF16),
        "sb_w_q": sb_w_q.astype(BF16),
        "sb_w_o": sb_w_o.astype(BF16),
        "kv_w": jnp.concatenate([kv_w_k, kv_w_v], axis=1).astype(BF16),
        "peer_wq_t": jnp.swapaxes(peer_w_q, 1, 2).astype(BF16),
        "peer_sk": peer_subkeys.reshape(depth, 2 * PEER_HEADS, PEER_N_KEYS, PEER_HALF).astype(BF16),
        "peer_u_t": jnp.swapaxes(peer_u, 1, 2).astype(BF16),
        "peer_v": peer_v.astype(BF16),
        "ln_g": ln_g,
        "ln_b": ln_b,
        "order": _cand_order(),
    }
    n_a = conv_w_in.shape[0]
    zero_conv = jnp.zeros((n_a, x_prompt.shape[0], state_conv.shape[2], x_prompt.shape[2]), x_prompt.dtype)
    y_p, conv_p, k_p, v_p = _trunk(x_prompt, zero_conv, None, None, 0, wts)
    y_s, conv_s, k_s, v_s = _trunk(x_sample, state_conv, cache_k, cache_v, cache_k.shape[1], wts)
    return (y_p, y_s, conv_p, k_p, v_p, conv_s, k_s, v_s)
```

```python
import functools
import math

import jax
import jax.numpy as jnp
from jax import lax
from jax.experimental import pallas as pl
from jax.experimental.pallas import tpu as pltpu

F32 = jnp.float32
BF16 = jnp.bfloat16
I32 = jnp.int32

LN_EPS = 1e-5
SB_HEAD_DIM = 64
PEER_HEADS = 8
PEER_N_KEYS = 128
PEER_TOPK = 16
PEER_HALF = 128
LANES = 128
G_PITCH = 136
VMEM_LIMIT = 56 * 1024 * 1024
EXP_UNDERFLOW = -104.0
G_PAIRS = 8


def _cparams(sem):
    return pltpu.CompilerParams(dimension_semantics=sem, vmem_limit_bytes=VMEM_LIMIT)


def _pack_bf16_pair(lo, hi):
    lo_bits = lax.bitcast_convert_type(lo.astype(BF16).astype(F32), jnp.uint32) >> 16
    hi_bits = lax.bitcast_convert_type(hi.astype(BF16).astype(F32), jnp.uint32) & jnp.uint32(0xFFFF0000)
    return lo_bits | hi_bits


def _unpack_bf16_pair(packed):
    lo = lax.bitcast_convert_type(packed << 16, F32)
    hi = lax.bitcast_convert_type(packed & jnp.uint32(0xFFFF0000), F32)
    return lo, hi


def _layernorm(v, g, b):
    mu = jnp.mean(v, axis=-1, keepdims=True)
    d = v - mu
    var = jnp.mean(d * d, axis=-1, keepdims=True)
    return d * lax.rsqrt(var + LN_EPS) * g + b


def _conv_kernel(x_ref, prev_ref, win_ref, wdw_ref, wout_ref, g_ref, b_ref,
                 o_ref, st_ref, carry_ref, *, alpha, n_seq_tiles):
    s = pl.program_id(1)
    x = x_ref[0]
    tm, d = x.shape
    proj = jnp.dot(x.astype(BF16), win_ref[...], preferred_element_type=F32)
    b_gate = proj[:, :d]
    u = proj[:, d:2 * d] * proj[:, 2 * d:]

    @pl.when(s == 0)
    def _():
        carry_ref[0:2, :] = prev_ref[0]

    p0 = carry_ref[0:1, :]
    p1 = carry_ref[1:2, :]
    row = lax.broadcasted_iota(I32, (tm, d), 0)
    um1 = jnp.where(row == 0, p1, pltpu.roll(u, 1, 0))
    um2 = jnp.where(row == 0, p0, jnp.where(row == 1, p1, pltpu.roll(u, 2, 0)))
    acc = wdw_ref[0:1, :] * um2 + wdw_ref[1:2, :] * um1 + wdw_ref[2:3, :] * u
    carry_ref[0:2, :] = u[tm - 2:tm, :]
    y = jnp.dot((b_gate * acc).astype(BF16), wout_ref[...], preferred_element_type=F32)
    o_ref[0] = _layernorm(alpha * x + y, g_ref[...], b_ref[...])

    @pl.when(s == n_seq_tiles - 1)
    def _():
        st_ref[0] = u[tm - 2:tm, :]


def _conv_layer(h, prev, w_in, w_dw, w_out, g, b, alpha):
    bsz, seq, d = h.shape
    tm = min(seq, 512)
    ns = seq // tm
    return pl.pallas_call(
        functools.partial(_conv_kernel, alpha=alpha, n_seq_tiles=ns),
        grid=(bsz, ns),
        in_specs=[
            pl.BlockSpec((1, tm, d), lambda i, j: (i, j, 0)),
            pl.BlockSpec((1, 2, d), lambda i, j: (i, 0, 0)),
            pl.BlockSpec((d, 3 * d), lambda i, j: (0, 0)),
            pl.BlockSpec((3, d), lambda i, j: (0, 0)),
            pl.BlockSpec((d, d), lambda i, j: (0, 0)),
            pl.BlockSpec((1, d), lambda i, j: (0, 0)),
            pl.BlockSpec((1, d), lambda i, j: (0, 0)),
        ],
        out_specs=[
            pl.BlockSpec((1, tm, d), lambda i, j: (i, j, 0)),
            pl.BlockSpec((1, 2, d), lambda i, j: (i, 0, 0)),
        ],
        out_shape=[
            jax.ShapeDtypeStruct((bsz, seq, d), F32),
            jax.ShapeDtypeStruct((bsz, 2, d), F32),
        ],
        scratch_shapes=[pltpu.VMEM((8, d), F32)],
        name="conv_mixer",
        compiler_params=_cparams(("arbitrary", "arbitrary")),
    )(h, prev, w_in, w_dw, w_out, g, b)


def _kv_kernel(x_ref, w_ref, k_ref, v_ref, kb_ref, vb_ref):
    d = x_ref.shape[1]
    y = jnp.dot(x_ref[...].astype(BF16), w_ref[...], preferred_element_type=F32)
    k = y[:, :d]
    v = y[:, d:]
    k_ref[...] = k
    v_ref[...] = v
    kb_ref[...] = k.astype(BF16)
    vb_ref[...] = v.astype(BF16)


def _kv_proj(h2, w_kv):
    t, d = h2.shape
    tm = min(t, 512)
    blk = pl.BlockSpec((tm, d), lambda i: (i, 0))
    return pl.pallas_call(
        _kv_kernel,
        grid=(t // tm,),
        in_specs=[blk, pl.BlockSpec((d, 2 * d), lambda i: (0, 0))],
        out_specs=[blk, blk, blk, blk],
        out_shape=[
            jax.ShapeDtypeStruct((t, d), F32), jax.ShapeDtypeStruct((t, d), F32),
            jax.ShapeDtypeStruct((t, d), BF16), jax.ShapeDtypeStruct((t, d), BF16),
        ],
        name="kv_proj",
        compiler_params=_cparams(("arbitrary",)),
    )(h2, w_kv)


def _q_kernel(x_ref, w_ref, o_ref, *, scale):
    o_ref[...] = (jnp.dot(x_ref[...].astype(BF16), w_ref[...],
                          preferred_element_type=F32) * scale).astype(o_ref.dtype)


def _q_proj(h2, w_q):
    t, d = h2.shape
    tm = min(t, 512)
    blk = pl.BlockSpec((tm, d), lambda i: (i, 0))
    scale = SB_HEAD_DIM ** -0.5
    assert math.frexp(scale)[0] == 0.5, scale
    return pl.pallas_call(
        functools.partial(_q_kernel, scale=scale),
        grid=(t // tm,),
        in_specs=[blk, pl.BlockSpec((d, d), lambda i: (0, 0))],
        out_specs=blk,
        out_shape=jax.ShapeDtypeStruct((t, d), BF16),
        name="q_proj",
        compiler_params=_cparams(("arbitrary",)),
    )(h2, w_q)


def _oproj_ln_kernel(a_ref, h_ref, w_ref, g_ref, b_ref, o_ref, *, alpha):
    y = jnp.dot(a_ref[...], w_ref[...], preferred_element_type=F32)
    o_ref[...] = _layernorm(alpha * h_ref[...] + y, g_ref[...], b_ref[...])


def _oproj_ln(a2, h2, w_o, g, b, alpha):
    t, d = h2.shape
    tm = min(t, 512)
    blk = pl.BlockSpec((tm, d), lambda i: (i, 0))
    vec = pl.BlockSpec((1, d), lambda i: (0, 0))
    return pl.pallas_call(
        functools.partial(_oproj_ln_kernel, alpha=alpha),
        grid=(t // tm,),
        in_specs=[blk, blk, pl.BlockSpec((d, d), lambda i: (0, 0)), vec, vec],
        out_specs=blk,
        out_shape=jax.ShapeDtypeStruct((t, d), F32),
        name="oproj_ln",
        compiler_params=_cparams(("arbitrary",)),
    )(a2, h2, w_o, g, b)


def _sb_kernel(q_ref, k_ref, v_ref, tri_ref, o_ref, *, q_pos0, tq, tk, n_kblocks):
    qi = pl.program_id(1)
    q2 = q_ref[0]
    lane = lax.broadcasted_iota(I32, (1, LANES), 1)
    first = lane < SB_HEAD_DIM
    zero = jnp.zeros_like(q2)
    q_heads = (jnp.where(first, q2, zero), jnp.where(first, zero, q2))
    q_start = q_pos0 + qi * tq
    qpos = q_start + lax.broadcasted_iota(I32, (tq, 1), 0)
    nblk = jnp.minimum(n_kblocks, (q_start + tq - 1 + tk - 1) // tk)

    def cond(carry):
        return jnp.logical_and(carry[0] < nblk, carry[1] > 0)

    def body(carry):
        j = carry[0]
        kb = nblk - 1 - j
        start = pl.multiple_of(kb * tk, tk)
        kblk = k_ref[0, pl.ds(start, tk), :]
        vblk = v_ref[0, pl.ds(start, tk), :]
        kpos = start + lax.broadcasted_iota(I32, (1, tk), 1)
        causal = kpos < qpos
        out = []
        for hd in range(2):
            acc, run = carry[2 + 2 * hd], carry[3 + 2 * hd]
            z = lax.dot_general(q_heads[hd], kblk, (((1,), (1,)), ((), ())),
                                preferred_element_type=F32)
            log_sig = jnp.minimum(z, 0.0) - jnp.log(1.0 + jnp.exp(-jnp.abs(z)))
            log_not = jnp.where(causal, log_sig - z, 0.0)
            suffix = jnp.dot(log_not.astype(BF16), tri_ref[...],
                             preferred_element_type=F32) + run
            a = jnp.where(causal, jnp.exp(log_sig + suffix), 0.0)
            acc = acc + jnp.dot(a.astype(BF16), vblk, preferred_element_type=F32)
            run = run + jnp.sum(log_not, axis=1, keepdims=True)
            out += [acc, run]
        live = (jnp.max(jnp.maximum(out[1], out[3])) > EXP_UNDERFLOW).astype(I32)
        return (j + 1, live, *out)

    init = (jnp.int32(0), jnp.int32(1),
            jnp.zeros((tq, LANES), F32), jnp.zeros((tq, 1), F32),
            jnp.zeros((tq, LANES), F32), jnp.zeros((tq, 1), F32))
    res = lax.while_loop(cond, body, init)
    o_ref[0] = jnp.where(first, res[2], res[4]).astype(o_ref.dtype)


def _sb_attention(q, k_all, v_all, q_pos0):
    bsz, sq, d = q.shape
    tq = min(sq, 256)
    tk = 256
    sk = k_all.shape[1]
    n_kblocks = sk // tk
    tri = (lax.broadcasted_iota(I32, (tk, tk), 0) >
           lax.broadcasted_iota(I32, (tk, tk), 1)).astype(BF16)
    return pl.pallas_call(
        functools.partial(_sb_kernel, q_pos0=q_pos0, tq=tq, tk=tk, n_kblocks=n_kblocks),
        grid=(bsz, sq // tq, d // LANES),
        in_specs=[
            pl.BlockSpec((1, tq, LANES), lambda b, i, p: (b, i, p)),
            pl.BlockSpec((1, sk, LANES), lambda b, i, p: (b, 0, p)),
            pl.BlockSpec((1, sk, LANES), lambda b, i, p: (b, 0, p)),
            pl.BlockSpec((tk, tk), lambda b, i, p: (0, 0)),
        ],
        out_specs=pl.BlockSpec((1, tq, LANES), lambda b, i, p: (b, i, p)),
        out_shape=jax.ShapeDtypeStruct((bsz, sq, d), BF16),
        name="sb_attention",
        compiler_params=_cparams(("arbitrary", "arbitrary", "arbitrary")),
    )(q, k_all, v_all, tri)


def _top16_rows(s):
    row = lax.broadcasted_iota(I32, s.shape, 0)
    vals, idxs = [], []
    for _ in range(PEER_TOPK):
        m = jnp.max(s, axis=0, keepdims=True)
        sel = jnp.min(jnp.where(s == m, row, PEER_N_KEYS), axis=0, keepdims=True)
        vals.append(m)
        idxs.append(sel)
        s = jnp.where(row == sel, -jnp.inf, s)
    return jnp.concatenate(vals, axis=0), jnp.concatenate(idxs, axis=0)


def _cand_rows(x1, x2):
    k = PEER_TOPK
    n = x1.shape[1]
    parts = [(jnp.broadcast_to(x1[0:1], (k, n)), x2)]
    for a in range(1, 8):
        parts.append((jnp.broadcast_to(x1[a:a + 1], (8, n)), x2[0:8]))
    parts.append((x1[8:16], jnp.broadcast_to(x2[0:1], (8, n))))
    return (jnp.concatenate([p[0] for p in parts], axis=0),
            jnp.concatenate([p[1] for p in parts], axis=0))


def _cand_order():
    k = PEER_TOPK
    pairs = [(0, b) for b in range(k)]
    for a in range(1, 8):
        pairs += [(a, b) for b in range(8)]
    pairs += [(a, 0) for a in range(8, k)]
    flat = [a * k + b if (a + 1) * (b + 1) <= k else 1 << 20 for a, b in pairs]
    return jnp.broadcast_to(jnp.asarray(flat, I32)[:, None], (len(flat), LANES))


def _product_top16(v1, n1, v2, n2, order, valid):
    c1, c2 = _cand_rows(v1, v2)
    cand = jnp.where(valid, c1 + c2, -jnp.inf)
    best, code = [], []
    for _ in range(PEER_TOPK):
        m = jnp.max(cand, axis=0, keepdims=True)
        sel = jnp.min(jnp.where(cand == m, order, 1 << 21), axis=0, keepdims=True)
        best.append(m)
        code.append(sel)
        cand = jnp.where(order == sel, -jnp.inf, cand)
    code = jnp.concatenate(code, axis=0)
    rank_a, rank_b = code >> 4, code & (PEER_TOPK - 1)
    bi = jnp.zeros_like(code)
    bj = jnp.zeros_like(code)
    for r in range(PEER_TOPK):
        bi = jnp.where(rank_a == r, n1[r:r + 1], bi)
        bj = jnp.where(rank_b == r, n2[r:r + 1], bj)
    return jnp.concatenate(best, axis=0), bi, bj


def _route_kernel(h_ref, wq_ref, sk_ref, order_ref, i_ref, j_ref, w_ref,
                  q_s, v_s, n_s, it_s, jt_s, wt_s):
    tr = h_ref.shape[0]
    nsub = tr // LANES
    x = h_ref[...].astype(BF16)
    q_t = lax.dot_general(wq_ref[...], x, (((1,), (1,)), ((), ())),
                          preferred_element_type=F32)
    q_s[...] = q_t.astype(BF16)

    def half_body(hp, _):
        qs = q_s[pl.ds(pl.multiple_of(hp * PEER_HALF, PEER_HALF), PEER_HALF), :]
        s_t = jnp.dot(sk_ref[hp], qs, preferred_element_type=F32)
        for sub in range(nsub):
            sl = slice(sub * LANES, (sub + 1) * LANES)
            v, n = _top16_rows(s_t[:, sl])
            v_s[hp, :, sl] = v
            n_s[hp, :, sl] = n
        return 0

    lax.fori_loop(0, 2 * PEER_HEADS, half_body, 0)

    order = order_ref[...]
    valid = order < (1 << 20)

    def head_body(hd, _):
        for sub in range(nsub):
            sl = slice(sub * LANES, (sub + 1) * LANES)
            v1, v2 = v_s[2 * hd, :, sl], v_s[2 * hd + 1, :, sl]
            n1, n2 = n_s[2 * hd, :, sl], n_s[2 * hd + 1, :, sl]
            best, bi, bj = _product_top16(v1, n1, v2, n2, order, valid)
            e = jnp.exp(best - best[0:1])
            gate = e / jnp.sum(e, axis=0, keepdims=True)
            rows = pl.ds(pl.multiple_of(hd * PEER_TOPK, PEER_TOPK), PEER_TOPK)
            it_s[rows, sl] = bi
            jt_s[rows, sl] = bj
            wt_s[rows, sl] = gate
        return 0

    lax.fori_loop(0, PEER_HEADS, head_body, 0)
    i_ref[...] = it_s[...].T
    j_ref[...] = jt_s[...].T
    w_ref[...] = wt_s[...].T


def _peer_route(h2, wq_t, sk, order):
    t, d = h2.shape
    tr = min(t, 512)
    assert t % tr == 0 and tr % LANES == 0, (t, tr)
    nq = wq_t.shape[0]
    hk = PEER_HEADS * PEER_TOPK
    blk = pl.BlockSpec((tr, hk), lambda i: (i, 0))
    return pl.pallas_call(
        _route_kernel,
        grid=(t // tr,),
        in_specs=[
            pl.BlockSpec((tr, d), lambda i: (i, 0)),
            pl.BlockSpec((nq, d), lambda i: (0, 0)),
            pl.BlockSpec(sk.shape, lambda i: (0, 0, 0)),
            pl.BlockSpec(order.shape, lambda i: (0, 0)),
        ],
        out_specs=[blk, blk, blk],
        out_shape=[jax.ShapeDtypeStruct((t, hk), I32), jax.ShapeDtypeStruct((t, hk), I32),
                   jax.ShapeDtypeStruct((t, hk), F32)],
        scratch_shapes=[
            pltpu.VMEM((nq, tr), BF16),
            pltpu.VMEM((2 * PEER_HEADS, PEER_TOPK, tr), F32),
            pltpu.VMEM((2 * PEER_HEADS, PEER_TOPK, tr), I32),
            pltpu.VMEM((hk, tr), I32),
            pltpu.VMEM((hk, tr), I32),
            pltpu.VMEM((hk, tr), F32),
        ],
        name="peer_route",
        compiler_params=_cparams(("arbitrary",)),
    )(h2, wq_t, sk, order)


def _expert_kernel(h_ref, i_ref, j_ref, w_ref, ut_ref, v_ref, g_ref, b_ref, o_ref,
                   xb_s, gate_s, acc_s, *, alpha, n_chunks, sub_e):
    c = pl.program_id(1)
    tm = h_ref.shape[0]
    half = tm // 2
    ec = ut_ref.shape[1]
    hk = i_ref.shape[1]

    @pl.when(c == 0)
    def _():
        xb_s[...] = h_ref[...].astype(BF16)
        acc_s[...] = jnp.zeros_like(acc_s)
        key = lax.broadcasted_iota(I32, (PEER_N_KEYS, hk), 0)
        zeros = jnp.zeros((PEER_N_KEYS, hk), BF16)

        def group_body(p, _):
            k0 = pl.multiple_of(p * G_PAIRS, G_PAIRS)
            k1 = pl.multiple_of(half + k0, G_PAIRS)
            i_blk = (i_ref[pl.ds(k0, G_PAIRS), :], i_ref[pl.ds(k1, G_PAIRS), :])
            j_blk = (j_ref[pl.ds(k0, G_PAIRS), :], j_ref[pl.ds(k1, G_PAIRS), :])
            w_blk = (w_ref[pl.ds(k0, G_PAIRS), :], w_ref[pl.ds(k1, G_PAIRS), :])
            for pair in range(G_PAIRS):
                lhs, rhs = [], []
                for side in range(2):
                    it = jnp.broadcast_to(i_blk[side][pair:pair + 1, :], (PEER_N_KEYS, hk))
                    jt = jnp.broadcast_to(j_blk[side][pair:pair + 1, :], (PEER_N_KEYS, hk))
                    wt = jnp.broadcast_to(w_blk[side][pair:pair + 1, :], (PEER_N_KEYS, hk))
                    lhs.append(jnp.where(key == it, 1.0, 0.0).astype(BF16))
                    rhs.append(jnp.where(key == jt, wt, 0.0).astype(BF16))
                x_mat = jnp.concatenate(lhs, axis=1)
                y_mat = jnp.concatenate(
                    [jnp.concatenate([rhs[0], zeros], axis=1),
                     jnp.concatenate([zeros, rhs[1]], axis=1)], axis=0)
                gp = lax.dot_general(x_mat, y_mat, (((1,), (1,)), ((), ())),
                                     preferred_element_type=F32)
                base = pl.multiple_of((k0 + pair) * G_PITCH, 8)
                gate_s[pl.ds(base, PEER_N_KEYS), :] = _pack_bf16_pair(gp[:, :LANES], gp[:, LANES:])
            return 0

        lax.fori_loop(0, half // G_PAIRS, group_body, 0)

    i_base = c * (ec // PEER_N_KEYS)
    per_sub = sub_e // PEER_N_KEYS
    for sc in range(ec // sub_e):
        a = jnp.dot(xb_s[...], ut_ref[:, sc * sub_e:(sc + 1) * sub_e],
                    preferred_element_type=F32)
        gate = jnp.concatenate(
            [jnp.concatenate(_unpack_bf16_pair(
                gate_s[pl.ds(i_base + sc * per_sub + r, half, stride=G_PITCH), :]), axis=0)
             for r in range(per_sub)], axis=1)
        act = 0.5 * a * (1.0 + lax.erf(a * (2.0 ** -0.5)))
        hm = (act * gate).astype(BF16)
        acc_s[...] += jnp.dot(hm, v_ref[sc * sub_e:(sc + 1) * sub_e, :],
                              preferred_element_type=F32)

    @pl.when(c == n_chunks - 1)
    def _():
        o_ref[...] = _layernorm(alpha * h_ref[...] + acc_s[...], g_ref[...], b_ref[...])


def _peer_experts(h2, ri, rj, rw, u_t, v_tab, g, b, alpha):
    t, d = h2.shape
    n_exp = v_tab.shape[0]
    tm = min(t, 512)
    ec = 2048
    assert t % tm == 0 and tm % (2 * G_PAIRS) == 0 and n_exp % ec == 0, (t, n_exp)
    nc = n_exp // ec
    hk = ri.shape[1]
    tok = pl.BlockSpec((tm, d), lambda i, c: (i, 0))
    rt = pl.BlockSpec((tm, hk), lambda i, c: (i, 0))
    vec = pl.BlockSpec((1, d), lambda i, c: (0, 0))
    return pl.pallas_call(
        functools.partial(_expert_kernel, alpha=alpha, n_chunks=nc, sub_e=512),
        grid=(t // tm, nc),
        in_specs=[tok, rt, rt, rt,
                  pl.BlockSpec((d, ec), lambda i, c: (0, c)),
                  pl.BlockSpec((ec, d), lambda i, c: (c, 0)),
                  vec, vec],
        out_specs=tok,
        out_shape=jax.ShapeDtypeStruct((t, d), F32),
        scratch_shapes=[
            pltpu.VMEM((tm, d), BF16),
            pltpu.VMEM((tm // 2 * G_PITCH, LANES), jnp.uint32),
            pltpu.VMEM((tm, d), F32),
        ],
        name="peer_experts",
        compiler_params=_cparams(("arbitrary", "arbitrary")),
    )(h2, ri, rj, rw, u_t, v_tab, g, b)


def _trunk(x, conv_prev, k_past, v_past, q_pos0, wts):
    bsz, seq, d = x.shape
    t = bsz * seq
    n_a = wts["conv_w_in"].shape[0]
    depth = n_a + wts["sb_w_q"].shape[0]
    alpha = (2.0 * depth) ** 0.25
    h = x
    new_conv = []
    k_new = v_new = k_all = v_all = None
    for layer in range(depth):
        g0, b0 = wts["ln_g"][layer, 0][None], wts["ln_b"][layer, 0][None]
        g1, b1 = wts["ln_g"][layer, 1][None], wts["ln_b"][layer, 1][None]
        if layer < n_a:
            h, st = _conv_layer(h, conv_prev[layer], wts["conv_w_in"][layer],
                                wts["conv_w_dw"][layer], wts["conv_w_out"][layer], g0, b0, alpha)
            new_conv.append(st)
        else:
            h2 = h.reshape(t, d)
            if layer == n_a:
                k2, v2, kb, vb = _kv_proj(h2, wts["kv_w"])
                k_new = k2.reshape(bsz, seq, d // SB_HEAD_DIM, SB_HEAD_DIM)
                v_new = v2.reshape(bsz, seq, d // SB_HEAD_DIM, SB_HEAD_DIM)
                k_all, v_all = kb.reshape(bsz, seq, d), vb.reshape(bsz, seq, d)
                if k_past is not None:
                    past = k_past.shape[1]
                    k_all = jnp.concatenate([k_past.reshape(bsz, past, d).astype(BF16), k_all], axis=1)
                    v_all = jnp.concatenate([v_past.reshape(bsz, past, d).astype(BF16), v_all], axis=1)
                pad = (-k_all.shape[1]) % 256
                if pad:
                    k_all = jnp.pad(k_all, ((0, 0), (0, pad), (0, 0)))
                    v_all = jnp.pad(v_all, ((0, 0), (0, pad), (0, 0)))
            j = layer - n_a
            q = _q_proj(h2, wts["sb_w_q"][j]).reshape(bsz, seq, d)
            att = _sb_attention(q, k_all, v_all, q_pos0)
            h = _oproj_ln(att.reshape(t, d), h2, wts["sb_w_o"][j], g0, b0, alpha).reshape(bsz, seq, d)
        h2 = h.reshape(t, d)
        ri, rj, rw = _peer_route(h2, wts["peer_wq_t"][layer], wts["peer_sk"][layer], wts["order"])
        h = _peer_experts(h2, ri, rj, rw, wts["peer_u_t"][layer], wts["peer_v"][layer],
                          g1, b1, alpha).reshape(bsz, seq, d)
    return h, jnp.stack(new_conv, axis=0), k_new, v_new


def kernel(x_prompt, x_sample, state_conv, cache_k, cache_v, conv_w_in, conv_w_dw, conv_w_out,
           sb_w_q, sb_w_o, kv_w_k, kv_w_v, peer_w_q, peer_subkeys, peer_u, peer_v, ln_g, ln_b):
    depth = peer_w_q.shape[0]
    wts = {
        "conv_w_in": conv_w_in.astype(BF16),
        "conv_w_dw": conv_w_dw,
        "conv_w_out": conv_w_out.astype(BF16),
        "sb_w_q": sb_w_q.astype(BF16),
        "sb_w_o": sb_w_o.astype(BF16),
        "kv_w": jnp.concatenate([kv_w_k, kv_w_v], axis=1).astype(BF16),
        "peer_wq_t": jnp.swapaxes(peer_w_q, 1, 2).astype(BF16),
        "peer_sk": peer_subkeys.reshape(depth, 2 * PEER_HEADS, PEER_N_KEYS, PEER_HALF).astype(BF16),
        "peer_u_t": jnp.swapaxes(peer_u, 1, 2).astype(BF16),
        "peer_v": peer_v.astype(BF16),
        "ln_g": ln_g,
        "ln_b": ln_b,
        "order": _cand_order(),
    }
    n_a = conv_w_in.shape[0]
    zero_conv = jnp.zeros((n_a, x_prompt.shape[0], state_conv.shape[2], x_prompt.shape[2]), x_prompt.dtype)
    y_p, conv_p, k_p, v_p = _trunk(x_prompt, zero_conv, None, None, 0, wts)
    y_s, conv_s, k_s, v_s = _trunk(x_sample, state_conv, cache_k, cache_v, cache_k.shape[1], wts)
    return (y_p, y_s, conv_p, k_p, v_p, conv_s, k_s, v_s)
```

```python
import functools
import math

import jax
import jax.numpy as jnp
from jax import lax
from jax.experimental import pallas as pl
from jax.experimental.pallas import tpu as pltpu

F32 = jnp.float32
BF16 = jnp.bfloat16
I32 = jnp.int32

LN_EPS = 1e-5
SB_HEAD_DIM = 64
PEER_HEADS = 8
PEER_N_KEYS = 128
PEER_TOPK = 16
PEER_HALF = 128
LANES = 128
G_PITCH = 136
VMEM_LIMIT = 56 * 1024 * 1024
EXP_UNDERFLOW = -104.0
G_PAIRS = 16
SB_HEADS_PER_STEP = 2


def _cparams(sem):
    return pltpu.CompilerParams(dimension_semantics=sem, vmem_limit_bytes=VMEM_LIMIT)


def _pack_bf16_pair(lo, hi):
    lo_bits = lax.bitcast_convert_type(lo.astype(BF16).astype(F32), jnp.uint32) >> 16
    hi_bits = lax.bitcast_convert_type(hi.astype(BF16).astype(F32), jnp.uint32) & jnp.uint32(0xFFFF0000)
    return lo_bits | hi_bits


def _unpack_bf16_pair(packed):
    lo = lax.bitcast_convert_type(packed << 16, F32)
    hi = lax.bitcast_convert_type(packed & jnp.uint32(0xFFFF0000), F32)
    return lo, hi


def _layernorm(v, g, b):
    mu = jnp.mean(v, axis=-1, keepdims=True)
    d = v - mu
    var = jnp.mean(d * d, axis=-1, keepdims=True)
    return d * lax.rsqrt(var + LN_EPS) * g + b


def _conv_kernel(x_ref, prev_ref, win_ref, wdw_ref, wout_ref, g_ref, b_ref,
                 o_ref, st_ref, carry_ref, *, alpha, n_seq_tiles):
    s = pl.program_id(1)
    x = x_ref[0]
    tm, d = x.shape
    proj = jnp.dot(x.astype(BF16), win_ref[...], preferred_element_type=F32)
    b_gate = proj[:, :d]
    u = proj[:, d:2 * d] * proj[:, 2 * d:]

    @pl.when(s == 0)
    def _():
        carry_ref[0:2, :] = prev_ref[0]

    p0 = carry_ref[0:1, :]
    p1 = carry_ref[1:2, :]
    row = lax.broadcasted_iota(I32, (tm, d), 0)
    um1 = jnp.where(row == 0, p1, pltpu.roll(u, 1, 0))
    um2 = jnp.where(row == 0, p0, jnp.where(row == 1, p1, pltpu.roll(u, 2, 0)))
    acc = wdw_ref[0:1, :] * um2 + wdw_ref[1:2, :] * um1 + wdw_ref[2:3, :] * u
    carry_ref[0:2, :] = u[tm - 2:tm, :]
    y = jnp.dot((b_gate * acc).astype(BF16), wout_ref[...], preferred_element_type=F32)
    o_ref[0] = _layernorm(alpha * x + y, g_ref[...], b_ref[...])

    @pl.when(s == n_seq_tiles - 1)
    def _():
        st_ref[0] = u[tm - 2:tm, :]


def _conv_layer(h, prev, w_in, w_dw, w_out, g, b, alpha):
    bsz, seq, d = h.shape
    tm = min(seq, 512)
    ns = seq // tm
    return pl.pallas_call(
        functools.partial(_conv_kernel, alpha=alpha, n_seq_tiles=ns),
        grid=(bsz, ns),
        in_specs=[
            pl.BlockSpec((1, tm, d), lambda i, j: (i, j, 0)),
            pl.BlockSpec((1, 2, d), lambda i, j: (i, 0, 0)),
            pl.BlockSpec((d, 3 * d), lambda i, j: (0, 0)),
            pl.BlockSpec((3, d), lambda i, j: (0, 0)),
            pl.BlockSpec((d, d), lambda i, j: (0, 0)),
            pl.BlockSpec((1, d), lambda i, j: (0, 0)),
            pl.BlockSpec((1, d), lambda i, j: (0, 0)),
        ],
        out_specs=[
            pl.BlockSpec((1, tm, d), lambda i, j: (i, j, 0)),
            pl.BlockSpec((1, 2, d), lambda i, j: (i, 0, 0)),
        ],
        out_shape=[
            jax.ShapeDtypeStruct((bsz, seq, d), F32),
            jax.ShapeDtypeStruct((bsz, 2, d), F32),
        ],
        scratch_shapes=[pltpu.VMEM((8, d), F32)],
        name="conv_mixer",
        compiler_params=_cparams(("arbitrary", "arbitrary")),
    )(h, prev, w_in, w_dw, w_out, g, b)


def _kv_kernel(x_ref, w_ref, k_ref, v_ref, kb_ref, vb_ref):
    d = x_ref.shape[1]
    y = jnp.dot(x_ref[...].astype(BF16), w_ref[...], preferred_element_type=F32)
    k = y[:, :d]
    v = y[:, d:]
    k_ref[...] = k
    v_ref[...] = v
    kb_ref[...] = k.astype(BF16)
    vb_ref[...] = v.astype(BF16)


def _kv_proj(h2, w_kv):
    t, d = h2.shape
    tm = min(t, 512)
    blk = pl.BlockSpec((tm, d), lambda i: (i, 0))
    return pl.pallas_call(
        _kv_kernel,
        grid=(t // tm,),
        in_specs=[blk, pl.BlockSpec((d, 2 * d), lambda i: (0, 0))],
        out_specs=[blk, blk, blk, blk],
        out_shape=[
            jax.ShapeDtypeStruct((t, d), F32), jax.ShapeDtypeStruct((t, d), F32),
            jax.ShapeDtypeStruct((t, d), BF16), jax.ShapeDtypeStruct((t, d), BF16),
        ],
        name="kv_proj",
        compiler_params=_cparams(("arbitrary",)),
    )(h2, w_kv)


def _q_kernel(x_ref, w_ref, o_ref, *, scale):
    o_ref[...] = (jnp.dot(x_ref[...].astype(BF16), w_ref[...],
                          preferred_element_type=F32) * scale).astype(o_ref.dtype)


def _q_proj(h2, w_q):
    t, d = h2.shape
    tm = min(t, 512)
    blk = pl.BlockSpec((tm, d), lambda i: (i, 0))
    scale = SB_HEAD_DIM ** -0.5
    assert math.frexp(scale)[0] == 0.5, scale
    return pl.pallas_call(
        functools.partial(_q_kernel, scale=scale),
        grid=(t // tm,),
        in_specs=[blk, pl.BlockSpec((d, d), lambda i: (0, 0))],
        out_specs=blk,
        out_shape=jax.ShapeDtypeStruct((t, d), BF16),
        name="q_proj",
        compiler_params=_cparams(("arbitrary",)),
    )(h2, w_q)


def _oproj_ln_kernel(a_ref, h_ref, w_ref, g_ref, b_ref, o_ref, *, alpha):
    y = jnp.dot(a_ref[...], w_ref[...], preferred_element_type=F32)
    o_ref[...] = _layernorm(alpha * h_ref[...] + y, g_ref[...], b_ref[...])


def _oproj_ln(a2, h2, w_o, g, b, alpha):
    t, d = h2.shape
    tm = min(t, 512)
    blk = pl.BlockSpec((tm, d), lambda i: (i, 0))
    vec = pl.BlockSpec((1, d), lambda i: (0, 0))
    return pl.pallas_call(
        functools.partial(_oproj_ln_kernel, alpha=alpha),
        grid=(t // tm,),
        in_specs=[blk, blk, pl.BlockSpec((d, d), lambda i: (0, 0)), vec, vec],
        out_specs=blk,
        out_shape=jax.ShapeDtypeStruct((t, d), F32),
        name="oproj_ln",
        compiler_params=_cparams(("arbitrary",)),
    )(a2, h2, w_o, g, b)


def _sb_kernel(q_ref, k_ref, v_ref, tri_ref, o_ref, *, q_pos0, tq, tk, n_kblocks):
    qi = pl.program_id(1)
    q_all = q_ref[0]
    width = q_all.shape[1]
    n_heads = width // SB_HEAD_DIM
    head_of_lane = lax.broadcasted_iota(I32, (1, width), 1) // SB_HEAD_DIM
    zero = jnp.zeros_like(q_all)
    q_heads = [jnp.where(head_of_lane == hd, q_all, zero) for hd in range(n_heads)]
    q_start = q_pos0 + qi * tq
    qpos = q_start + lax.broadcasted_iota(I32, (tq, 1), 0)
    nblk = jnp.minimum(n_kblocks, (q_start + tq - 1 + tk - 1) // tk)

    def cond(carry):
        return jnp.logical_and(carry[0] < nblk, carry[1] > 0)

    def body(carry):
        j = carry[0]
        kb = nblk - 1 - j
        start = pl.multiple_of(kb * tk, tk)
        kblk = k_ref[0, pl.ds(start, tk), :]
        vblk = v_ref[0, pl.ds(start, tk), :]
        kpos = start + lax.broadcasted_iota(I32, (1, tk), 1)
        causal = kpos < qpos
        out = []
        for hd in range(n_heads):
            acc, run = carry[2 + 2 * hd], carry[3 + 2 * hd]
            z = lax.dot_general(q_heads[hd], kblk, (((1,), (1,)), ((), ())),
                                preferred_element_type=F32)
            log_sig = jnp.minimum(z, 0.0) - jnp.log(1.0 + jnp.exp(-jnp.abs(z)))
            log_not = jnp.where(causal, log_sig - z, 0.0)
            suffix = jnp.dot(log_not.astype(BF16), tri_ref[...],
                             preferred_element_type=F32) + run
            a = jnp.where(causal, jnp.exp(log_sig + suffix), 0.0)
            acc = acc + jnp.dot(a.astype(BF16), vblk, preferred_element_type=F32)
            run = run + jnp.sum(log_not, axis=1, keepdims=True)
            out += [acc, run]
        live = (jnp.max(functools.reduce(jnp.maximum, out[1::2])) > EXP_UNDERFLOW).astype(I32)
        return (j + 1, live, *out)

    init = (jnp.int32(0), jnp.int32(1)) + (jnp.zeros((tq, width), F32), jnp.zeros((tq, 1), F32)) * n_heads
    res = lax.while_loop(cond, body, init)
    result = res[2]
    for hd in range(1, n_heads):
        result = jnp.where(head_of_lane == hd, res[2 + 2 * hd], result)
    o_ref[0] = result.astype(o_ref.dtype)


def _sb_attention(q, k_all, v_all, q_pos0):
    bsz, sq, d = q.shape
    tq = min(sq, 256)
    tk = 256
    sk = k_all.shape[1]
    n_kblocks = sk // tk
    tri = (lax.broadcasted_iota(I32, (tk, tk), 0) >
           lax.broadcasted_iota(I32, (tk, tk), 1)).astype(BF16)
    width = SB_HEADS_PER_STEP * SB_HEAD_DIM
    assert d % width == 0 and width % LANES == 0, (d, width)
    return pl.pallas_call(
        functools.partial(_sb_kernel, q_pos0=q_pos0, tq=tq, tk=tk, n_kblocks=n_kblocks),
        grid=(bsz, sq // tq, d // width),
        in_specs=[
            pl.BlockSpec((1, tq, width), lambda b, i, p: (b, i, p)),
            pl.BlockSpec((1, sk, width), lambda b, i, p: (b, 0, p)),
            pl.BlockSpec((1, sk, width), lambda b, i, p: (b, 0, p)),
            pl.BlockSpec((tk, tk), lambda b, i, p: (0, 0)),
        ],
        out_specs=pl.BlockSpec((1, tq, width), lambda b, i, p: (b, i, p)),
        out_shape=jax.ShapeDtypeStruct((bsz, sq, d), BF16),
        name="sb_attention",
        compiler_params=_cparams(("arbitrary", "arbitrary", "arbitrary")),
    )(q, k_all, v_all, tri)


def _sort_network(n):
    pairs = []
    p = 1
    while p < n:
        k = p
        while k >= 1:
            for j in range(k % p, n - k, 2 * k):
                for i in range(min(k, n - j - k)):
                    if (i + j) // (2 * p) == (i + j + k) // (2 * p):
                        pairs.append((i + j, i + j + k))
            k //= 2
        p *= 2
    return pairs


def _pop_merge(lists, ranks, steps, big, singles=None):
    lists, ranks = list(lists), list(ranks)
    vals, ids = [], []
    for step in range(steps):
        head, head_rank = lists[0], ranks[0]
        m = jnp.max(head, axis=0, keepdims=True)
        if singles is not None:
            m = jnp.maximum(m, jnp.max(singles[0], axis=0, keepdims=True))
        sel = jnp.min(jnp.where(head == m, head_rank, big), axis=0, keepdims=True)
        if singles is not None:
            sel = jnp.minimum(sel, jnp.min(jnp.where(singles[0] == m, singles[1], big),
                                           axis=0, keepdims=True))
            singles = (jnp.where(singles[1] == sel, -jnp.inf, singles[0]), singles[1])
        vals.append(m)
        ids.append(sel)
        won = head_rank == sel
        for r in range(min(steps - 1 - step, len(lists) - 1)):
            lists[r] = jnp.where(won, lists[r + 1], lists[r])
            ranks[r] = jnp.where(won, ranks[r + 1], ranks[r])
    return jnp.concatenate(vals, axis=0), jnp.concatenate(ids, axis=0)


def _top16_rows(s):
    n_slabs = s.shape[0] // 8
    sub = lax.broadcasted_iota(I32, (8, s.shape[1]), 0)
    vals = [s[8 * r:8 * r + 8] for r in range(n_slabs)]
    keys = [sub + 8 * r for r in range(n_slabs)]
    for i, j in _sort_network(n_slabs):
        a, b, ka, kb = vals[i], vals[j], keys[i], keys[j]
        swap = (b > a) | ((b == a) & (kb < ka))
        vals[i], vals[j] = jnp.where(swap, b, a), jnp.where(swap, a, b)
        keys[i], keys[j] = jnp.where(swap, kb, ka), jnp.where(swap, ka, kb)
    return _pop_merge(vals, keys, PEER_TOPK, PEER_N_KEYS)


def _product_top16(v1, n1, v2, n2):
    k = PEER_TOPK
    a_low = lax.broadcasted_iota(I32, (8, v1.shape[1]), 0)
    lists, ranks = [], []
    for b in range(k):
        longest = k // (b + 1)
        lists.append(jnp.where(a_low < longest, v1[0:8] + v2[b:b + 1], -jnp.inf))
        ranks.append(a_low * k + b)
    singles = (v1[8:k] + v2[0:1], (a_low + 8) * k)
    best, code = _pop_merge(lists, ranks, k, k * k, singles)
    rank_a, rank_b = code >> 4, code & (PEER_TOPK - 1)
    bi = jnp.zeros_like(code)
    bj = jnp.zeros_like(code)
    for r in range(PEER_TOPK):
        bi = jnp.where(rank_a == r, n1[r:r + 1], bi)
        bj = jnp.where(rank_b == r, n2[r:r + 1], bj)
    return best, bi, bj


def _route_kernel(h_ref, wq_ref, sk_ref, i_ref, j_ref, w_ref,
                  q_s, v_s, n_s, it_s, jt_s, wt_s):
    tr = h_ref.shape[0]
    nsub = tr // LANES
    x = h_ref[...].astype(BF16)
    q_t = lax.dot_general(wq_ref[...], x, (((1,), (1,)), ((), ())),
                          preferred_element_type=F32)
    q_s[...] = q_t.astype(BF16)

    def half_body(hp, _):
        qs = q_s[pl.ds(pl.multiple_of(hp * PEER_HALF, PEER_HALF), PEER_HALF), :]
        s_t = jnp.dot(sk_ref[hp], qs, preferred_element_type=F32)
        for sub in range(nsub):
            sl = slice(sub * LANES, (sub + 1) * LANES)
            v, n = _top16_rows(s_t[:, sl])
            v_s[hp, :, sl] = v
            n_s[hp, :, sl] = n
        return 0

    lax.fori_loop(0, 2 * PEER_HEADS, half_body, 0)

    def head_body(hd, _):
        for sub in range(nsub):
            sl = slice(sub * LANES, (sub + 1) * LANES)
            v1, v2 = v_s[2 * hd, :, sl], v_s[2 * hd + 1, :, sl]
            n1, n2 = n_s[2 * hd, :, sl], n_s[2 * hd + 1, :, sl]
            best, bi, bj = _product_top16(v1, n1, v2, n2)
            e = jnp.exp(best - best[0:1])
            gate = e / jnp.sum(e, axis=0, keepdims=True)
            rows = pl.ds(pl.multiple_of(hd * PEER_TOPK, PEER_TOPK), PEER_TOPK)
            it_s[rows, sl] = bi
            jt_s[rows, sl] = bj
            wt_s[rows, sl] = gate
        return 0

    lax.fori_loop(0, PEER_HEADS, head_body, 0)
    i_ref[...] = it_s[...].T
    j_ref[...] = jt_s[...].T
    w_ref[...] = wt_s[...].T


def _peer_route(h2, wq_t, sk):
    t, d = h2.shape
    tr = min(t, 1024)
    assert t % tr == 0 and tr % LANES == 0, (t, tr)
    nq = wq_t.shape[0]
    hk = PEER_HEADS * PEER_TOPK
    blk = pl.BlockSpec((tr, hk), lambda i: (i, 0))
    return pl.pallas_call(
        _route_kernel,
        grid=(t // tr,),
        in_specs=[
            pl.BlockSpec((tr, d), lambda i: (i, 0)),
            pl.BlockSpec((nq, d), lambda i: (0, 0)),
            pl.BlockSpec(sk.shape, lambda i: (0, 0, 0)),
        ],
        out_specs=[blk, blk, blk],
        out_shape=[jax.ShapeDtypeStruct((t, hk), I32), jax.ShapeDtypeStruct((t, hk), I32),
                   jax.ShapeDtypeStruct((t, hk), F32)],
        scratch_shapes=[
            pltpu.VMEM((nq, tr), BF16),
            pltpu.VMEM((2 * PEER_HEADS, PEER_TOPK, tr), F32),
            pltpu.VMEM((2 * PEER_HEADS, PEER_TOPK, tr), I32),
            pltpu.VMEM((hk, tr), I32),
            pltpu.VMEM((hk, tr), I32),
            pltpu.VMEM((hk, tr), F32),
        ],
        name="peer_route",
        compiler_params=_cparams(("arbitrary",)),
    )(h2, wq_t, sk)


def _expert_kernel(h_ref, i_ref, j_ref, w_ref, ut_ref, v_ref, g_ref, b_ref, o_ref,
                   xb_s, gate_s, acc_s, *, alpha, n_chunks, sub_e):
    c = pl.program_id(1)
    tm = h_ref.shape[0]
    half = tm // 2
    ec = ut_ref.shape[1]
    hk = i_ref.shape[1]

    @pl.when(c == 0)
    def _():
        xb_s[...] = h_ref[...].astype(BF16)
        acc_s[...] = jnp.zeros_like(acc_s)
        key = lax.broadcasted_iota(I32, (PEER_N_KEYS, hk), 0)
        zeros = jnp.zeros((PEER_N_KEYS, hk), BF16)

        def group_body(p, _):
            k0 = pl.multiple_of(p * G_PAIRS, G_PAIRS)
            k1 = pl.multiple_of(half + k0, G_PAIRS)
            i_blk = (i_ref[pl.ds(k0, G_PAIRS), :], i_ref[pl.ds(k1, G_PAIRS), :])
            j_blk = (j_ref[pl.ds(k0, G_PAIRS), :], j_ref[pl.ds(k1, G_PAIRS), :])
            w_blk = (w_ref[pl.ds(k0, G_PAIRS), :], w_ref[pl.ds(k1, G_PAIRS), :])
            for pair in range(G_PAIRS):
                lhs, rhs = [], []
                for side in range(2):
                    it = jnp.broadcast_to(i_blk[side][pair:pair + 1, :], (PEER_N_KEYS, hk))
                    jt = jnp.broadcast_to(j_blk[side][pair:pair + 1, :], (PEER_N_KEYS, hk))
                    wt = jnp.broadcast_to(w_blk[side][pair:pair + 1, :], (PEER_N_KEYS, hk))
                    lhs.append(jnp.where(key == it, 1.0, 0.0).astype(BF16))
                    rhs.append(jnp.where(key == jt, wt, 0.0).astype(BF16))
                x_mat = jnp.concatenate(lhs, axis=1)
                y_mat = jnp.concatenate(
                    [jnp.concatenate([rhs[0], zeros], axis=1),
                     jnp.concatenate([zeros, rhs[1]], axis=1)], axis=0)
                gp = lax.dot_general(x_mat, y_mat, (((1,), (1,)), ((), ())),
                                     preferred_element_type=F32)
                base = pl.multiple_of((k0 + pair) * G_PITCH, 8)
                gate_s[pl.ds(base, PEER_N_KEYS), :] = _pack_bf16_pair(gp[:, :LANES], gp[:, LANES:])
            return 0

        lax.fori_loop(0, half // G_PAIRS, group_body, 0)

    i_base = c * (ec // PEER_N_KEYS)
    per_sub = sub_e // PEER_N_KEYS
    for sc in range(ec // sub_e):
        a = jnp.dot(xb_s[...], ut_ref[:, sc * sub_e:(sc + 1) * sub_e],
                    preferred_element_type=F32)
        gate = jnp.concatenate(
            [jnp.concatenate(_unpack_bf16_pair(
                gate_s[pl.ds(i_base + sc * per_sub + r, half, stride=G_PITCH), :]), axis=0)
             for r in range(per_sub)], axis=1)
        act = 0.5 * a * (1.0 + lax.erf(a * (2.0 ** -0.5)))
        hm = (act * gate).astype(BF16)
        acc_s[...] += jnp.dot(hm, v_ref[sc * sub_e:(sc + 1) * sub_e, :],
                              preferred_element_type=F32)

    @pl.when(c == n_chunks - 1)
    def _():
        o_ref[...] = _layernorm(alpha * h_ref[...] + acc_s[...], g_ref[...], b_ref[...])


def _peer_experts(h2, ri, rj, rw, u_t, v_tab, g, b, alpha):
    t, d = h2.shape
    n_exp = v_tab.shape[0]
    tm = min(t, 512)
    ec = 2048
    assert t % tm == 0 and tm % (2 * G_PAIRS) == 0 and n_exp % ec == 0, (t, n_exp)
    nc = n_exp // ec
    hk = ri.shape[1]
    tok = pl.BlockSpec((tm, d), lambda i, c: (i, 0))
    rt = pl.BlockSpec((tm, hk), lambda i, c: (i, 0))
    vec = pl.BlockSpec((1, d), lambda i, c: (0, 0))
    return pl.pallas_call(
        functools.partial(_expert_kernel, alpha=alpha, n_chunks=nc, sub_e=1024),
        grid=(t // tm, nc),
        in_specs=[tok, rt, rt, rt,
                  pl.BlockSpec((d, ec), lambda i, c: (0, c)),
                  pl.BlockSpec((ec, d), lambda i, c: (c, 0)),
                  vec, vec],
        out_specs=tok,
        out_shape=jax.ShapeDtypeStruct((t, d), F32),
        scratch_shapes=[
            pltpu.VMEM((tm, d), BF16),
            pltpu.VMEM((tm // 2 * G_PITCH, LANES), jnp.uint32),
            pltpu.VMEM((tm, d), F32),
        ],
        name="peer_experts",
        compiler_params=_cparams(("arbitrary", "arbitrary")),
    )(h2, ri, rj, rw, u_t, v_tab, g, b)


def _trunk(x, conv_prev, k_past, v_past, q_pos0, wts):
    bsz, seq, d = x.shape
    t = bsz * seq
    n_a = wts["conv_w_in"].shape[0]
    depth = n_a + wts["sb_w_q"].shape[0]
    alpha = (2.0 * depth) ** 0.25
    h = x
    new_conv = []
    k_new = v_new = k_all = v_all = None
    for layer in range(depth):
        g0, b0 = wts["ln_g"][layer, 0][None], wts["ln_b"][layer, 0][None]
        g1, b1 = wts["ln_g"][layer, 1][None], wts["ln_b"][layer, 1][None]
        if layer < n_a:
            h, st = _conv_layer(h, conv_prev[layer], wts["conv_w_in"][layer],
                                wts["conv_w_dw"][layer], wts["conv_w_out"][layer], g0, b0, alpha)
            new_conv.append(st)
        else:
            h2 = h.reshape(t, d)
            if layer == n_a:
                k2, v2, kb, vb = _kv_proj(h2, wts["kv_w"])
                k_new = k2.reshape(bsz, seq, d // SB_HEAD_DIM, SB_HEAD_DIM)
                v_new = v2.reshape(bsz, seq, d // SB_HEAD_DIM, SB_HEAD_DIM)
                k_all, v_all = kb.reshape(bsz, seq, d), vb.reshape(bsz, seq, d)
                if k_past is not None:
                    past = k_past.shape[1]
                    k_all = jnp.concatenate([k_past.reshape(bsz, past, d).astype(BF16), k_all], axis=1)
                    v_all = jnp.concatenate([v_past.reshape(bsz, past, d).astype(BF16), v_all], axis=1)
                pad = (-k_all.shape[1]) % 256
                if pad:
                    k_all = jnp.pad(k_all, ((0, 0), (0, pad), (0, 0)))
                    v_all = jnp.pad(v_all, ((0, 0), (0, pad), (0, 0)))
            j = layer - n_a
            q = _q_proj(h2, wts["sb_w_q"][j]).reshape(bsz, seq, d)
            att = _sb_attention(q, k_all, v_all, q_pos0)
            h = _oproj_ln(att.reshape(t, d), h2, wts["sb_w_o"][j], g0, b0, alpha).reshape(bsz, seq, d)
        h2 = h.reshape(t, d)
        ri, rj, rw = _peer_route(h2, wts["peer_wq_t"][layer], wts["peer_sk"][layer])
        h = _peer_experts(h2, ri, rj, rw, wts["peer_u_t"][layer], wts["peer_v"][layer],
                          g1, b1, alpha).reshape(bsz, seq, d)
    return h, jnp.stack(new_conv, axis=0), k_new, v_new


def kernel(x_prompt, x_sample, state_conv, cache_k, cache_v, conv_w_in, conv_w_dw, conv_w_out,
           sb_w_q, sb_w_o, kv_w_k, kv_w_v, peer_w_q, peer_subkeys, peer_u, peer_v, ln_g, ln_b):
    depth = peer_w_q.shape[0]
    wts = {
        "conv_w_in": conv_w_in.astype(BF16),
        "conv_w_dw": conv_w_dw,
        "conv_w_out": conv_w_out.astype(BF16),
        "sb_w_q": sb_w_q.astype(BF16),
        "sb_w_o": sb_w_o.astype(BF16),
        "kv_w": jnp.concatenate([kv_w_k, kv_w_v], axis=1).astype(BF16),
        "peer_wq_t": jnp.swapaxes(peer_w_q, 1, 2).astype(BF16),
        "peer_sk": peer_subkeys.reshape(depth, 2 * PEER_HEADS, PEER_N_KEYS, PEER_HALF).astype(BF16),
        "peer_u_t": jnp.swapaxes(peer_u, 1, 2).astype(BF16),
        "peer_v": peer_v.astype(BF16),
        "ln_g": ln_g,
        "ln_b": ln_b,
    }
    n_a = conv_w_in.shape[0]
    zero_conv = jnp.zeros((n_a, x_prompt.shape[0], state_conv.shape[2], x_prompt.shape[2]), x_prompt.dtype)
    y_p, conv_p, k_p, v_p = _trunk(x_prompt, zero_conv, None, None, 0, wts)
    y_s, conv_s, k_s, v_s = _trunk(x_sample, state_conv, cache_k, cache_v, cache_k.shape[1], wts)
    return (y_p, y_s, conv_p, k_p, v_p, conv_s, k_s, v_s)
```

```python
import functools
import math

import jax
import jax.numpy as jnp
from jax import lax
from jax.experimental import pallas as pl
from jax.experimental.pallas import tpu as pltpu

F32 = jnp.float32
BF16 = jnp.bfloat16
I32 = jnp.int32

LN_EPS = 1e-5
SB_HEAD_DIM = 64
PEER_HEADS = 8
PEER_N_KEYS = 128
PEER_TOPK = 16
PEER_HALF = 128
LANES = 128
G_PITCH = 136
VMEM_LIMIT = 56 * 1024 * 1024
EXP_UNDERFLOW = -104.0
G_PAIRS = 32
SB_HEADS_PER_STEP = 2


def _cparams(sem):
    return pltpu.CompilerParams(dimension_semantics=sem, vmem_limit_bytes=VMEM_LIMIT)


def _pack_bf16_pair(lo, hi):
    lo_bits = lax.bitcast_convert_type(lo.astype(BF16).astype(F32), jnp.uint32) >> 16
    hi_bits = lax.bitcast_convert_type(hi.astype(BF16).astype(F32), jnp.uint32) & jnp.uint32(0xFFFF0000)
    return lo_bits | hi_bits


def _unpack_bf16_pair(packed):
    lo = lax.bitcast_convert_type(packed << 16, F32)
    hi = lax.bitcast_convert_type(packed & jnp.uint32(0xFFFF0000), F32)
    return lo, hi


def _layernorm(v, g, b):
    mu = jnp.mean(v, axis=-1, keepdims=True)
    d = v - mu
    var = jnp.mean(d * d, axis=-1, keepdims=True)
    return d * lax.rsqrt(var + LN_EPS) * g + b


def _conv_kernel(x_ref, prev_ref, win_ref, wdw_ref, wout_ref, g_ref, b_ref,
                 o_ref, st_ref, carry_ref, *, alpha, n_seq_tiles):
    s = pl.program_id(1)
    x = x_ref[0]
    tm, d = x.shape
    proj = jnp.dot(x.astype(BF16), win_ref[...], preferred_element_type=F32)
    b_gate = proj[:, :d]
    u = proj[:, d:2 * d] * proj[:, 2 * d:]

    @pl.when(s == 0)
    def _():
        carry_ref[0:2, :] = prev_ref[0]

    p0 = carry_ref[0:1, :]
    p1 = carry_ref[1:2, :]
    row = lax.broadcasted_iota(I32, (tm, d), 0)
    um1 = jnp.where(row == 0, p1, pltpu.roll(u, 1, 0))
    um2 = jnp.where(row == 0, p0, jnp.where(row == 1, p1, pltpu.roll(u, 2, 0)))
    acc = wdw_ref[0:1, :] * um2 + wdw_ref[1:2, :] * um1 + wdw_ref[2:3, :] * u
    carry_ref[0:2, :] = u[tm - 2:tm, :]
    y = jnp.dot((b_gate * acc).astype(BF16), wout_ref[...], preferred_element_type=F32)
    o_ref[0] = _layernorm(alpha * x + y, g_ref[...], b_ref[...])

    @pl.when(s == n_seq_tiles - 1)
    def _():
        st_ref[0] = u[tm - 2:tm, :]


def _conv_layer(h, prev, w_in, w_dw, w_out, g, b, alpha):
    bsz, seq, d = h.shape
    tm = min(seq, 512)
    ns = seq // tm
    return pl.pallas_call(
        functools.partial(_conv_kernel, alpha=alpha, n_seq_tiles=ns),
        grid=(bsz, ns),
        in_specs=[
            pl.BlockSpec((1, tm, d), lambda i, j: (i, j, 0)),
            pl.BlockSpec((1, 2, d), lambda i, j: (i, 0, 0)),
            pl.BlockSpec((d, 3 * d), lambda i, j: (0, 0)),
            pl.BlockSpec((3, d), lambda i, j: (0, 0)),
            pl.BlockSpec((d, d), lambda i, j: (0, 0)),
            pl.BlockSpec((1, d), lambda i, j: (0, 0)),
            pl.BlockSpec((1, d), lambda i, j: (0, 0)),
        ],
        out_specs=[
            pl.BlockSpec((1, tm, d), lambda i, j: (i, j, 0)),
            pl.BlockSpec((1, 2, d), lambda i, j: (i, 0, 0)),
        ],
        out_shape=[
            jax.ShapeDtypeStruct((bsz, seq, d), F32),
            jax.ShapeDtypeStruct((bsz, 2, d), F32),
        ],
        scratch_shapes=[pltpu.VMEM((8, d), F32)],
        name="conv_mixer",
        compiler_params=_cparams(("arbitrary", "arbitrary")),
    )(h, prev, w_in, w_dw, w_out, g, b)


def _kv_kernel(x_ref, w_ref, k_ref, v_ref, kb_ref, vb_ref):
    d = x_ref.shape[1]
    y = jnp.dot(x_ref[...].astype(BF16), w_ref[...], preferred_element_type=F32)
    k = y[:, :d]
    v = y[:, d:]
    k_ref[...] = k
    v_ref[...] = v
    kb_ref[...] = k.astype(BF16)
    vb_ref[...] = v.astype(BF16)


def _kv_proj(h2, w_kv):
    t, d = h2.shape
    tm = min(t, 512)
    blk = pl.BlockSpec((tm, d), lambda i: (i, 0))
    return pl.pallas_call(
        _kv_kernel,
        grid=(t // tm,),
        in_specs=[blk, pl.BlockSpec((d, 2 * d), lambda i: (0, 0))],
        out_specs=[blk, blk, blk, blk],
        out_shape=[
            jax.ShapeDtypeStruct((t, d), F32), jax.ShapeDtypeStruct((t, d), F32),
            jax.ShapeDtypeStruct((t, d), BF16), jax.ShapeDtypeStruct((t, d), BF16),
        ],
        name="kv_proj",
        compiler_params=_cparams(("arbitrary",)),
    )(h2, w_kv)


def _q_kernel(x_ref, w_ref, o_ref, *, scale):
    o_ref[...] = (jnp.dot(x_ref[...].astype(BF16), w_ref[...],
                          preferred_element_type=F32) * scale).astype(o_ref.dtype)


def _q_proj(h2, w_q):
    t, d = h2.shape
    tm = min(t, 512)
    blk = pl.BlockSpec((tm, d), lambda i: (i, 0))
    scale = SB_HEAD_DIM ** -0.5
    assert math.frexp(scale)[0] == 0.5, scale
    return pl.pallas_call(
        functools.partial(_q_kernel, scale=scale),
        grid=(t // tm,),
        in_specs=[blk, pl.BlockSpec((d, d), lambda i: (0, 0))],
        out_specs=blk,
        out_shape=jax.ShapeDtypeStruct((t, d), BF16),
        name="q_proj",
        compiler_params=_cparams(("arbitrary",)),
    )(h2, w_q)


def _oproj_ln_kernel(a_ref, h_ref, w_ref, g_ref, b_ref, o_ref, *, alpha):
    y = jnp.dot(a_ref[...], w_ref[...], preferred_element_type=F32)
    o_ref[...] = _layernorm(alpha * h_ref[...] + y, g_ref[...], b_ref[...])


def _oproj_ln(a2, h2, w_o, g, b, alpha):
    t, d = h2.shape
    tm = min(t, 512)
    blk = pl.BlockSpec((tm, d), lambda i: (i, 0))
    vec = pl.BlockSpec((1, d), lambda i: (0, 0))
    return pl.pallas_call(
        functools.partial(_oproj_ln_kernel, alpha=alpha),
        grid=(t // tm,),
        in_specs=[blk, blk, pl.BlockSpec((d, d), lambda i: (0, 0)), vec, vec],
        out_specs=blk,
        out_shape=jax.ShapeDtypeStruct((t, d), F32),
        name="oproj_ln",
        compiler_params=_cparams(("arbitrary",)),
    )(a2, h2, w_o, g, b)


def _sb_kernel(q_ref, k_ref, v_ref, tri_ref, o_ref, *, q_pos0, tq, tk, n_kblocks):
    qi = pl.program_id(1)
    q_all = q_ref[0]
    width = q_all.shape[1]
    n_heads = width // SB_HEAD_DIM
    head_of_lane = lax.broadcasted_iota(I32, (1, width), 1) // SB_HEAD_DIM
    zero = jnp.zeros_like(q_all)
    q_heads = [jnp.where(head_of_lane == hd, q_all, zero) for hd in range(n_heads)]
    q_start = q_pos0 + qi * tq
    qpos = q_start + lax.broadcasted_iota(I32, (tq, 1), 0)
    nblk = jnp.minimum(n_kblocks, (q_start + tq - 1 + tk - 1) // tk)

    def cond(carry):
        return jnp.logical_and(carry[0] < nblk, carry[1] > 0)

    def body(carry):
        j = carry[0]
        kb = nblk - 1 - j
        start = pl.multiple_of(kb * tk, tk)
        kblk = k_ref[0, pl.ds(start, tk), :]
        vblk = v_ref[0, pl.ds(start, tk), :]
        kpos = start + lax.broadcasted_iota(I32, (1, tk), 1)
        causal = kpos < qpos
        out = []
        for hd in range(n_heads):
            acc, run = carry[2 + 2 * hd], carry[3 + 2 * hd]
            z = lax.dot_general(q_heads[hd], kblk, (((1,), (1,)), ((), ())),
                                preferred_element_type=F32)
            log_sig = jnp.minimum(z, 0.0) - jnp.log(1.0 + jnp.exp(-jnp.abs(z)))
            log_not = jnp.where(causal, log_sig - z, 0.0)
            suffix = jnp.dot(log_not.astype(BF16), tri_ref[...],
                             preferred_element_type=F32) + run
            a = jnp.where(causal, jnp.exp(log_sig + suffix), 0.0)
            acc = acc + jnp.dot(a.astype(BF16), vblk, preferred_element_type=F32)
            run = run + jnp.sum(log_not, axis=1, keepdims=True)
            out += [acc, run]
        live = (jnp.max(functools.reduce(jnp.maximum, out[1::2])) > EXP_UNDERFLOW).astype(I32)
        return (j + 1, live, *out)

    init = (jnp.int32(0), jnp.int32(1)) + (jnp.zeros((tq, width), F32), jnp.zeros((tq, 1), F32)) * n_heads
    res = lax.while_loop(cond, body, init)
    result = res[2]
    for hd in range(1, n_heads):
        result = jnp.where(head_of_lane == hd, res[2 + 2 * hd], result)
    o_ref[0] = result.astype(o_ref.dtype)


def _sb_attention(q, k_all, v_all, q_pos0):
    bsz, sq, d = q.shape
    tq = min(sq, 256)
    tk = 256
    sk = k_all.shape[1]
    n_kblocks = sk // tk
    tri = (lax.broadcasted_iota(I32, (tk, tk), 0) >
           lax.broadcasted_iota(I32, (tk, tk), 1)).astype(BF16)
    width = SB_HEADS_PER_STEP * SB_HEAD_DIM
    assert d % width == 0 and width % LANES == 0, (d, width)
    return pl.pallas_call(
        functools.partial(_sb_kernel, q_pos0=q_pos0, tq=tq, tk=tk, n_kblocks=n_kblocks),
        grid=(bsz, sq // tq, d // width),
        in_specs=[
            pl.BlockSpec((1, tq, width), lambda b, i, p: (b, i, p)),
            pl.BlockSpec((1, sk, width), lambda b, i, p: (b, 0, p)),
            pl.BlockSpec((1, sk, width), lambda b, i, p: (b, 0, p)),
            pl.BlockSpec((tk, tk), lambda b, i, p: (0, 0)),
        ],
        out_specs=pl.BlockSpec((1, tq, width), lambda b, i, p: (b, i, p)),
        out_shape=jax.ShapeDtypeStruct((bsz, sq, d), BF16),
        name="sb_attention",
        compiler_params=_cparams(("arbitrary", "arbitrary", "arbitrary")),
    )(q, k_all, v_all, tri)


def _sort_network(n):
    pairs = []
    p = 1
    while p < n:
        k = p
        while k >= 1:
            for j in range(k % p, n - k, 2 * k):
                for i in range(min(k, n - j - k)):
                    if (i + j) // (2 * p) == (i + j + k) // (2 * p):
                        pairs.append((i + j, i + j + k))
            k //= 2
        p *= 2
    return pairs


def _pop_merge(lists, ranks, steps, big, singles=None):
    lists, ranks = list(lists), list(ranks)
    vals, ids = [], []
    for step in range(steps):
        head, head_rank = lists[0], ranks[0]
        m = jnp.max(head, axis=0, keepdims=True)
        if singles is not None:
            m = jnp.maximum(m, jnp.max(singles[0], axis=0, keepdims=True))
        sel = jnp.min(jnp.where(head == m, head_rank, big), axis=0, keepdims=True)
        if singles is not None:
            sel = jnp.minimum(sel, jnp.min(jnp.where(singles[0] == m, singles[1], big),
                                           axis=0, keepdims=True))
            singles = (jnp.where(singles[1] == sel, -jnp.inf, singles[0]), singles[1])
        vals.append(m)
        ids.append(sel)
        won = head_rank == sel
        for r in range(min(steps - 1 - step, len(lists) - 1)):
            lists[r] = jnp.where(won, lists[r + 1], lists[r])
            ranks[r] = jnp.where(won, ranks[r + 1], ranks[r])
    return jnp.concatenate(vals, axis=0), jnp.concatenate(ids, axis=0)


def _top16_rows(s):
    n_slabs = s.shape[0] // 8
    sub = lax.broadcasted_iota(I32, (8, s.shape[1]), 0)
    vals = [s[8 * r:8 * r + 8] for r in range(n_slabs)]
    keys = [sub + 8 * r for r in range(n_slabs)]
    for i, j in _sort_network(n_slabs):
        a, b, ka, kb = vals[i], vals[j], keys[i], keys[j]
        swap = (b > a) | ((b == a) & (kb < ka))
        vals[i], vals[j] = jnp.where(swap, b, a), jnp.where(swap, a, b)
        keys[i], keys[j] = jnp.where(swap, kb, ka), jnp.where(swap, ka, kb)
    return _pop_merge(vals, keys, PEER_TOPK, PEER_N_KEYS)


def _product_top16(v1, n1, v2, n2):
    k = PEER_TOPK
    a_low = lax.broadcasted_iota(I32, (8, v1.shape[1]), 0)
    lists, ranks = [], []
    for b in range(k):
        longest = k // (b + 1)
        lists.append(jnp.where(a_low < longest, v1[0:8] + v2[b:b + 1], -jnp.inf))
        ranks.append(a_low * k + b)
    singles = (v1[8:k] + v2[0:1], (a_low + 8) * k)
    best, code = _pop_merge(lists, ranks, k, k * k, singles)
    rank_a, rank_b = code >> 4, code & (PEER_TOPK - 1)
    bi = jnp.zeros_like(code)
    bj = jnp.zeros_like(code)
    for r in range(PEER_TOPK):
        bi = jnp.where(rank_a == r, n1[r:r + 1], bi)
        bj = jnp.where(rank_b == r, n2[r:r + 1], bj)
    return best, bi, bj


def _route_kernel(h_ref, wq_ref, sk_ref, i_ref, j_ref, w_ref,
                  q_s, v_s, n_s, it_s, jt_s, wt_s):
    tr = h_ref.shape[0]
    nsub = tr // LANES
    x = h_ref[...].astype(BF16)
    q_t = lax.dot_general(wq_ref[...], x, (((1,), (1,)), ((), ())),
                          preferred_element_type=F32)
    q_s[...] = q_t.astype(BF16)

    def half_body(hp, _):
        qs = q_s[pl.ds(pl.multiple_of(hp * PEER_HALF, PEER_HALF), PEER_HALF), :]
        s_t = jnp.dot(sk_ref[hp], qs, preferred_element_type=F32)
        for sub in range(nsub):
            sl = slice(sub * LANES, (sub + 1) * LANES)
            v, n = _top16_rows(s_t[:, sl])
            v_s[hp, :, sl] = v
            n_s[hp, :, sl] = n
        return 0

    lax.fori_loop(0, 2 * PEER_HEADS, half_body, 0)

    def head_body(hd, _):
        for sub in range(nsub):
            sl = slice(sub * LANES, (sub + 1) * LANES)
            v1, v2 = v_s[2 * hd, :, sl], v_s[2 * hd + 1, :, sl]
            n1, n2 = n_s[2 * hd, :, sl], n_s[2 * hd + 1, :, sl]
            best, bi, bj = _product_top16(v1, n1, v2, n2)
            e = jnp.exp(best - best[0:1])
            gate = e / jnp.sum(e, axis=0, keepdims=True)
            rows = pl.ds(pl.multiple_of(hd * PEER_TOPK, PEER_TOPK), PEER_TOPK)
            it_s[rows, sl] = bi
            jt_s[rows, sl] = bj
            wt_s[rows, sl] = gate
        return 0

    lax.fori_loop(0, PEER_HEADS, head_body, 0)
    i_ref[...] = it_s[...].T
    j_ref[...] = jt_s[...].T
    w_ref[...] = wt_s[...].T


def _peer_route(h2, wq_t, sk):
    t, d = h2.shape
    tr = min(t, 1024)
    assert t % tr == 0 and tr % LANES == 0, (t, tr)
    nq = wq_t.shape[0]
    hk = PEER_HEADS * PEER_TOPK
    blk = pl.BlockSpec((tr, hk), lambda i: (i, 0))
    return pl.pallas_call(
        _route_kernel,
        grid=(t // tr,),
        in_specs=[
            pl.BlockSpec((tr, d), lambda i: (i, 0)),
            pl.BlockSpec((nq, d), lambda i: (0, 0)),
            pl.BlockSpec(sk.shape, lambda i: (0, 0, 0)),
        ],
        out_specs=[blk, blk, blk],
        out_shape=[jax.ShapeDtypeStruct((t, hk), I32), jax.ShapeDtypeStruct((t, hk), I32),
                   jax.ShapeDtypeStruct((t, hk), F32)],
        scratch_shapes=[
            pltpu.VMEM((nq, tr), BF16),
            pltpu.VMEM((2 * PEER_HEADS, PEER_TOPK, tr), F32),
            pltpu.VMEM((2 * PEER_HEADS, PEER_TOPK, tr), I32),
            pltpu.VMEM((hk, tr), I32),
            pltpu.VMEM((hk, tr), I32),
            pltpu.VMEM((hk, tr), F32),
        ],
        name="peer_route",
        compiler_params=_cparams(("arbitrary",)),
    )(h2, wq_t, sk)


def _expert_kernel(h_ref, i_ref, j_ref, w_ref, u_ref, v_ref, g_ref, b_ref, o_ref,
                   xb_s, gate_s, acc_s, *, alpha, n_chunks, sub_e):
    c = pl.program_id(1)
    tm = h_ref.shape[0]
    half = tm // 2
    ec = u_ref.shape[0]
    hk = i_ref.shape[1]

    @pl.when(c == 0)
    def _():
        xb_s[...] = h_ref[...].astype(BF16)
        acc_s[...] = jnp.zeros_like(acc_s)
        key = lax.broadcasted_iota(I32, (PEER_N_KEYS, hk), 0)
        zeros = jnp.zeros((PEER_N_KEYS, hk), BF16)

        def group_body(p, _):
            k0 = pl.multiple_of(p * G_PAIRS, G_PAIRS)
            k1 = pl.multiple_of(half + k0, G_PAIRS)
            i_blk = (i_ref[pl.ds(k0, G_PAIRS), :], i_ref[pl.ds(k1, G_PAIRS), :])
            j_blk = (j_ref[pl.ds(k0, G_PAIRS), :], j_ref[pl.ds(k1, G_PAIRS), :])
            w_blk = (w_ref[pl.ds(k0, G_PAIRS), :], w_ref[pl.ds(k1, G_PAIRS), :])
            for pair in range(G_PAIRS):
                lhs, rhs = [], []
                for side in range(2):
                    it = jnp.broadcast_to(i_blk[side][pair:pair + 1, :], (PEER_N_KEYS, hk))
                    jt = jnp.broadcast_to(j_blk[side][pair:pair + 1, :], (PEER_N_KEYS, hk))
                    wt = jnp.broadcast_to(w_blk[side][pair:pair + 1, :], (PEER_N_KEYS, hk))
                    lhs.append(jnp.where(key == it, 1.0, 0.0).astype(BF16))
                    rhs.append(jnp.where(key == jt, wt, 0.0).astype(BF16))
                x_mat = jnp.concatenate(lhs, axis=1)
                y_mat = jnp.concatenate(
                    [jnp.concatenate([rhs[0], zeros], axis=1),
                     jnp.concatenate([zeros, rhs[1]], axis=1)], axis=0)
                gp = lax.dot_general(x_mat, y_mat, (((1,), (1,)), ((), ())),
                                     preferred_element_type=F32)
                base = pl.multiple_of((k0 + pair) * G_PITCH, 8)
                gate_s[pl.ds(base, PEER_N_KEYS), :] = _pack_bf16_pair(gp[:, :LANES], gp[:, LANES:])
            return 0

        lax.fori_loop(0, half // G_PAIRS, group_body, 0)

    i_base = c * (ec // PEER_N_KEYS)
    per_sub = sub_e // PEER_N_KEYS
    for sc in range(ec // sub_e):
        a = lax.dot_general(xb_s[...], u_ref[sc * sub_e:(sc + 1) * sub_e, :], (((1,), (1,)), ((), ())),
                            preferred_element_type=F32)
        gate = jnp.concatenate(
            [jnp.concatenate(_unpack_bf16_pair(
                gate_s[pl.ds(i_base + sc * per_sub + r, half, stride=G_PITCH), :]), axis=0)
             for r in range(per_sub)], axis=1)
        act = 0.5 * a * (1.0 + lax.erf(a * (2.0 ** -0.5)))
        hm = (act * gate).astype(BF16)
        acc_s[...] += jnp.dot(hm, v_ref[sc * sub_e:(sc + 1) * sub_e, :],
                              preferred_element_type=F32)

    @pl.when(c == n_chunks - 1)
    def _():
        o_ref[...] = _layernorm(alpha * h_ref[...] + acc_s[...], g_ref[...], b_ref[...])


def _peer_experts(h2, ri, rj, rw, u_tab, v_tab, g, b, alpha):
    t, d = h2.shape
    n_exp = v_tab.shape[0]
    tm = min(t, 512)
    ec = 2048
    assert t % tm == 0 and tm % (2 * G_PAIRS) == 0 and n_exp % ec == 0, (t, n_exp)
    nc = n_exp // ec
    hk = ri.shape[1]
    tok = pl.BlockSpec((tm, d), lambda i, c: (i, 0))
    rt = pl.BlockSpec((tm, hk), lambda i, c: (i, 0))
    vec = pl.BlockSpec((1, d), lambda i, c: (0, 0))
    return pl.pallas_call(
        functools.partial(_expert_kernel, alpha=alpha, n_chunks=nc, sub_e=1024),
        grid=(t // tm, nc),
        in_specs=[tok, rt, rt, rt,
                  pl.BlockSpec((ec, d), lambda i, c: (c, 0)),
                  pl.BlockSpec((ec, d), lambda i, c: (c, 0)),
                  vec, vec],
        out_specs=tok,
        out_shape=jax.ShapeDtypeStruct((t, d), F32),
        scratch_shapes=[
            pltpu.VMEM((tm, d), BF16),
            pltpu.VMEM((tm // 2 * G_PITCH, LANES), jnp.uint32),
            pltpu.VMEM((tm, d), F32),
        ],
        name="peer_experts",
        compiler_params=_cparams(("arbitrary", "arbitrary")),
    )(h2, ri, rj, rw, u_tab, v_tab, g, b)


def _trunk(x, conv_prev, k_past, v_past, q_pos0, wts):
    bsz, seq, d = x.shape
    t = bsz * seq
    n_a = wts["conv_w_in"].shape[0]
    depth = n_a + wts["sb_w_q"].shape[0]
    alpha = (2.0 * depth) ** 0.25
    h = x
    new_conv = []
    k_new = v_new = k_all = v_all = None
    for layer in range(depth):
        g0, b0 = wts["ln_g"][layer, 0][None], wts["ln_b"][layer, 0][None]
        g1, b1 = wts["ln_g"][layer, 1][None], wts["ln_b"][layer, 1][None]
        if layer < n_a:
            h, st = _conv_layer(h, conv_prev[layer], wts["conv_w_in"][layer],
                                wts["conv_w_dw"][layer], wts["conv_w_out"][layer], g0, b0, alpha)
            new_conv.append(st)
        else:
            h2 = h.reshape(t, d)
            if layer == n_a:
                k2, v2, kb, vb = _kv_proj(h2, wts["kv_w"])
                k_new = k2.reshape(bsz, seq, d // SB_HEAD_DIM, SB_HEAD_DIM)
                v_new = v2.reshape(bsz, seq, d // SB_HEAD_DIM, SB_HEAD_DIM)
                k_all, v_all = kb.reshape(bsz, seq, d), vb.reshape(bsz, seq, d)
                if k_past is not None:
                    past = k_past.shape[1]
                    k_all = jnp.concatenate([k_past.reshape(bsz, past, d).astype(BF16), k_all], axis=1)
                    v_all = jnp.concatenate([v_past.reshape(bsz, past, d).astype(BF16), v_all], axis=1)
                pad = (-k_all.shape[1]) % 256
                if pad:
                    k_all = jnp.pad(k_all, ((0, 0), (0, pad), (0, 0)))
                    v_all = jnp.pad(v_all, ((0, 0), (0, pad), (0, 0)))
            j = layer - n_a
            q = _q_proj(h2, wts["sb_w_q"][j]).reshape(bsz, seq, d)
            att = _sb_attention(q, k_all, v_all, q_pos0)
            h = _oproj_ln(att.reshape(t, d), h2, wts["sb_w_o"][j], g0, b0, alpha).reshape(bsz, seq, d)
        h2 = h.reshape(t, d)
        ri, rj, rw = _peer_route(h2, wts["peer_wq_t"][layer], wts["peer_sk"][layer])
        h = _peer_experts(h2, ri, rj, rw, wts["peer_u"][layer], wts["peer_v"][layer],
                          g1, b1, alpha).reshape(bsz, seq, d)
    return h, jnp.stack(new_conv, axis=0), k_new, v_new


def kernel(x_prompt, x_sample, state_conv, cache_k, cache_v, conv_w_in, conv_w_dw, conv_w_out,
           sb_w_q, sb_w_o, kv_w_k, kv_w_v, peer_w_q, peer_subkeys, peer_u, peer_v, ln_g, ln_b):
    depth = peer_w_q.shape[0]
    wts = {
        "conv_w_in": conv_w_in.astype(BF16),
        "conv_w_dw": conv_w_dw,
        "conv_w_out": conv_w_out.astype(BF16),
        "sb_w_q": sb_w_q.astype(BF16),
        "sb_w_o": sb_w_o.astype(BF16),
        "kv_w": jnp.concatenate([kv_w_k, kv_w_v], axis=1).astype(BF16),
        "peer_wq_t": jnp.swapaxes(peer_w_q, 1, 2).astype(BF16),
        "peer_sk": peer_subkeys.reshape(depth, 2 * PEER_HEADS, PEER_N_KEYS, PEER_HALF).astype(BF16),
        "peer_u": peer_u.astype(BF16),
        "peer_v": peer_v.astype(BF16),
        "ln_g": ln_g,
        "ln_b": ln_b,
    }
    n_a = conv_w_in.shape[0]
    zero_conv = jnp.zeros((n_a, x_prompt.shape[0], state_conv.shape[2], x_prompt.shape[2]), x_prompt.dtype)
    y_p, conv_p, k_p, v_p = _trunk(x_prompt, zero_conv, None, None, 0, wts)
    y_s, conv_s, k_s, v_s = _trunk(x_sample, state_conv, cache_k, cache_v, cache_k.shape[1], wts)
    return (y_p, y_s, conv_p, k_p, v_p, conv_s, k_s, v_s)
```

```python
import functools
import math

import jax
import jax.numpy as jnp
from jax import lax
from jax.experimental import pallas as pl
from jax.experimental.pallas import tpu as pltpu

F32 = jnp.float32
BF16 = jnp.bfloat16
I32 = jnp.int32

LN_EPS = 1e-5
SB_HEAD_DIM = 64
PEER_HEADS = 8
PEER_N_KEYS = 128
PEER_TOPK = 16
PEER_HALF = 128
LANES = 128
G_PITCH = 136
VMEM_LIMIT = 56 * 1024 * 1024
EXP_UNDERFLOW = -104.0
G_PAIRS = 32
SB_HEADS_PER_STEP = 2


def _cparams(sem):
    return pltpu.CompilerParams(dimension_semantics=sem, vmem_limit_bytes=VMEM_LIMIT)


def _pack_bf16_pair(lo, hi):
    lo_bits = lax.bitcast_convert_type(lo.astype(BF16).astype(F32), jnp.uint32) >> 16
    hi_bits = lax.bitcast_convert_type(hi.astype(BF16).astype(F32), jnp.uint32) & jnp.uint32(0xFFFF0000)
    return lo_bits | hi_bits


def _unpack_bf16_pair(packed):
    lo = lax.bitcast_convert_type(packed << 16, F32)
    hi = lax.bitcast_convert_type(packed & jnp.uint32(0xFFFF0000), F32)
    return lo, hi


def _layernorm(v, g, b):
    mu = jnp.mean(v, axis=-1, keepdims=True)
    d = v - mu
    var = jnp.mean(d * d, axis=-1, keepdims=True)
    return d * lax.rsqrt(var + LN_EPS) * g + b


def _conv_kernel(x_ref, prev_ref, win_ref, wdw_ref, wout_ref, g_ref, b_ref,
                 o_ref, st_ref, carry_ref, *, alpha, n_seq_tiles):
    s = pl.program_id(1)
    x = x_ref[0]
    tm, d = x.shape
    proj = jnp.dot(x.astype(BF16), win_ref[...], preferred_element_type=F32)
    b_gate = proj[:, :d]
    u = proj[:, d:2 * d] * proj[:, 2 * d:]

    @pl.when(s == 0)
    def _():
        carry_ref[0:2, :] = prev_ref[0]

    p0 = carry_ref[0:1, :]
    p1 = carry_ref[1:2, :]
    row = lax.broadcasted_iota(I32, (tm, d), 0)
    um1 = jnp.where(row == 0, p1, pltpu.roll(u, 1, 0))
    um2 = jnp.where(row == 0, p0, jnp.where(row == 1, p1, pltpu.roll(u, 2, 0)))
    acc = wdw_ref[0:1, :] * um2 + wdw_ref[1:2, :] * um1 + wdw_ref[2:3, :] * u
    carry_ref[0:2, :] = u[tm - 2:tm, :]
    y = jnp.dot((b_gate * acc).astype(BF16), wout_ref[...], preferred_element_type=F32)
    o_ref[0] = _layernorm(alpha * x + y, g_ref[...], b_ref[...])

    @pl.when(s == n_seq_tiles - 1)
    def _():
        st_ref[0] = u[tm - 2:tm, :]


def _conv_layer(h, prev, w_in, w_dw, w_out, g, b, alpha):
    bsz, seq, d = h.shape
    tm = min(seq, 512)
    ns = seq // tm
    return pl.pallas_call(
        functools.partial(_conv_kernel, alpha=alpha, n_seq_tiles=ns),
        grid=(bsz, ns),
        in_specs=[
            pl.BlockSpec((1, tm, d), lambda i, j: (i, j, 0)),
            pl.BlockSpec((1, 2, d), lambda i, j: (i, 0, 0)),
            pl.BlockSpec((d, 3 * d), lambda i, j: (0, 0)),
            pl.BlockSpec((3, d), lambda i, j: (0, 0)),
            pl.BlockSpec((d, d), lambda i, j: (0, 0)),
            pl.BlockSpec((1, d), lambda i, j: (0, 0)),
            pl.BlockSpec((1, d), lambda i, j: (0, 0)),
        ],
        out_specs=[
            pl.BlockSpec((1, tm, d), lambda i, j: (i, j, 0)),
            pl.BlockSpec((1, 2, d), lambda i, j: (i, 0, 0)),
        ],
        out_shape=[
            jax.ShapeDtypeStruct((bsz, seq, d), F32),
            jax.ShapeDtypeStruct((bsz, 2, d), F32),
        ],
        scratch_shapes=[pltpu.VMEM((8, d), F32)],
        name="conv_mixer",
        compiler_params=_cparams(("arbitrary", "arbitrary")),
    )(h, prev, w_in, w_dw, w_out, g, b)


def _kv_kernel(x_ref, w_ref, k_ref, v_ref, kb_ref, vb_ref):
    d = x_ref.shape[1]
    y = jnp.dot(x_ref[...].astype(BF16), w_ref[...], preferred_element_type=F32)
    k = y[:, :d]
    v = y[:, d:]
    k_ref[...] = k
    v_ref[...] = v
    kb_ref[...] = k.astype(BF16)
    vb_ref[...] = v.astype(BF16)


def _kv_proj(h2, w_kv):
    t, d = h2.shape
    tm = min(t, 512)
    blk = pl.BlockSpec((tm, d), lambda i: (i, 0))
    return pl.pallas_call(
        _kv_kernel,
        grid=(t // tm,),
        in_specs=[blk, pl.BlockSpec((d, 2 * d), lambda i: (0, 0))],
        out_specs=[blk, blk, blk, blk],
        out_shape=[
            jax.ShapeDtypeStruct((t, d), F32), jax.ShapeDtypeStruct((t, d), F32),
            jax.ShapeDtypeStruct((t, d), BF16), jax.ShapeDtypeStruct((t, d), BF16),
        ],
        name="kv_proj",
        compiler_params=_cparams(("arbitrary",)),
    )(h2, w_kv)


def _q_kernel(x_ref, w_ref, o_ref, *, scale):
    o_ref[...] = (jnp.dot(x_ref[...].astype(BF16), w_ref[...],
                          preferred_element_type=F32) * scale).astype(o_ref.dtype)


def _q_proj(h2, w_q):
    t, d = h2.shape
    tm = min(t, 512)
    blk = pl.BlockSpec((tm, d), lambda i: (i, 0))
    scale = SB_HEAD_DIM ** -0.5
    assert math.frexp(scale)[0] == 0.5, scale
    return pl.pallas_call(
        functools.partial(_q_kernel, scale=scale),
        grid=(t // tm,),
        in_specs=[blk, pl.BlockSpec((d, d), lambda i: (0, 0))],
        out_specs=blk,
        out_shape=jax.ShapeDtypeStruct((t, d), BF16),
        name="q_proj",
        compiler_params=_cparams(("arbitrary",)),
    )(h2, w_q)


def _oproj_ln_kernel(a_ref, h_ref, w_ref, g_ref, b_ref, o_ref, *, alpha):
    y = jnp.dot(a_ref[...], w_ref[...], preferred_element_type=F32)
    o_ref[...] = _layernorm(alpha * h_ref[...] + y, g_ref[...], b_ref[...])


def _oproj_ln(a2, h2, w_o, g, b, alpha):
    t, d = h2.shape
    tm = min(t, 512)
    blk = pl.BlockSpec((tm, d), lambda i: (i, 0))
    vec = pl.BlockSpec((1, d), lambda i: (0, 0))
    return pl.pallas_call(
        functools.partial(_oproj_ln_kernel, alpha=alpha),
        grid=(t // tm,),
        in_specs=[blk, blk, pl.BlockSpec((d, d), lambda i: (0, 0)), vec, vec],
        out_specs=blk,
        out_shape=jax.ShapeDtypeStruct((t, d), F32),
        name="oproj_ln",
        compiler_params=_cparams(("arbitrary",)),
    )(a2, h2, w_o, g, b)


def _sb_kernel(q_ref, k_ref, v_ref, tri_ref, o_ref, *, q_pos0, tq, tk, n_kblocks, width):
    for group in range(q_ref.shape[2] // width):
        _sb_head_group(q_ref, k_ref, v_ref, tri_ref, o_ref, slice(group * width, (group + 1) * width),
                       q_pos0=q_pos0, tq=tq, tk=tk, n_kblocks=n_kblocks)


def _sb_head_group(q_ref, k_ref, v_ref, tri_ref, o_ref, lanes, *, q_pos0, tq, tk, n_kblocks):
    qi = pl.program_id(1)
    q_all = q_ref[0, :, lanes]
    width = q_all.shape[1]
    n_heads = width // SB_HEAD_DIM
    head_of_lane = lax.broadcasted_iota(I32, (1, width), 1) // SB_HEAD_DIM
    zero = jnp.zeros_like(q_all)
    q_heads = [jnp.where(head_of_lane == hd, q_all, zero) for hd in range(n_heads)]
    q_start = q_pos0 + qi * tq
    qpos = q_start + lax.broadcasted_iota(I32, (tq, 1), 0)
    nblk = jnp.minimum(n_kblocks, (q_start + tq - 1 + tk - 1) // tk)

    def cond(carry):
        return jnp.logical_and(carry[0] < nblk, carry[1] > 0)

    def body(carry):
        j = carry[0]
        kb = nblk - 1 - j
        start = pl.multiple_of(kb * tk, tk)
        kblk = k_ref[0, pl.ds(start, tk), lanes]
        vblk = v_ref[0, pl.ds(start, tk), lanes]
        kpos = start + lax.broadcasted_iota(I32, (1, tk), 1)
        causal = kpos < qpos
        out = []
        for hd in range(n_heads):
            acc, run = carry[2 + 2 * hd], carry[3 + 2 * hd]
            z = lax.dot_general(q_heads[hd], kblk, (((1,), (1,)), ((), ())),
                                preferred_element_type=F32)
            log_sig = jnp.minimum(z, 0.0) - jnp.log(1.0 + jnp.exp(-jnp.abs(z)))
            log_not = jnp.where(causal, log_sig - z, 0.0)
            suffix = jnp.dot(log_not.astype(BF16), tri_ref[...],
                             preferred_element_type=F32) + run
            a = jnp.where(causal, jnp.exp(log_sig + suffix), 0.0)
            acc = acc + jnp.dot(a.astype(BF16), vblk, preferred_element_type=F32)
            run = run + jnp.sum(log_not, axis=1, keepdims=True)
            out += [acc, run]
        live = (jnp.max(functools.reduce(jnp.maximum, out[1::2])) > EXP_UNDERFLOW).astype(I32)
        return (j + 1, live, *out)

    init = (jnp.int32(0), jnp.int32(1)) + (jnp.zeros((tq, width), F32), jnp.zeros((tq, 1), F32)) * n_heads
    res = lax.while_loop(cond, body, init)
    result = res[2]
    for hd in range(1, n_heads):
        result = jnp.where(head_of_lane == hd, res[2 + 2 * hd], result)
    o_ref[0, :, lanes] = result.astype(o_ref.dtype)


def _sb_attention(q, k_all, v_all, q_pos0):
    bsz, sq, d = q.shape
    tq = min(sq, 256)
    tk = 256
    sk = k_all.shape[1]
    n_kblocks = sk // tk
    tri = (lax.broadcasted_iota(I32, (tk, tk), 0) >
           lax.broadcasted_iota(I32, (tk, tk), 1)).astype(BF16)
    width = SB_HEADS_PER_STEP * SB_HEAD_DIM
    assert d % width == 0 and width % LANES == 0, (d, width)
    return pl.pallas_call(
        functools.partial(_sb_kernel, q_pos0=q_pos0, tq=tq, tk=tk, n_kblocks=n_kblocks, width=width),
        grid=(bsz, sq // tq),
        in_specs=[
            pl.BlockSpec((1, tq, d), lambda b, i: (b, i, 0)),
            pl.BlockSpec((1, sk, d), lambda b, i: (b, 0, 0)),
            pl.BlockSpec((1, sk, d), lambda b, i: (b, 0, 0)),
            pl.BlockSpec((tk, tk), lambda b, i: (0, 0)),
        ],
        out_specs=pl.BlockSpec((1, tq, d), lambda b, i: (b, i, 0)),
        out_shape=jax.ShapeDtypeStruct((bsz, sq, d), BF16),
        name="sb_attention",
        compiler_params=_cparams(("arbitrary", "arbitrary")),
    )(q, k_all, v_all, tri)


def _sort_network(n):
    pairs = []
    p = 1
    while p < n:
        k = p
        while k >= 1:
            for j in range(k % p, n - k, 2 * k):
                for i in range(min(k, n - j - k)):
                    if (i + j) // (2 * p) == (i + j + k) // (2 * p):
                        pairs.append((i + j, i + j + k))
            k //= 2
        p *= 2
    return pairs


def _pop_merge(lists, ranks, steps, big, singles=None):
    lists, ranks = list(lists), list(ranks)
    vals, ids = [], []
    for step in range(steps):
        head, head_rank = lists[0], ranks[0]
        m = jnp.max(head, axis=0, keepdims=True)
        if singles is not None:
            m = jnp.maximum(m, jnp.max(singles[0], axis=0, keepdims=True))
        sel = jnp.min(jnp.where(head == m, head_rank, big), axis=0, keepdims=True)
        if singles is not None:
            sel = jnp.minimum(sel, jnp.min(jnp.where(singles[0] == m, singles[1], big),
                                           axis=0, keepdims=True))
            singles = (jnp.where(singles[1] == sel, -jnp.inf, singles[0]), singles[1])
        vals.append(m)
        ids.append(sel)
        won = head_rank == sel
        for r in range(min(steps - 1 - step, len(lists) - 1)):
            lists[r] = jnp.where(won, lists[r + 1], lists[r])
            ranks[r] = jnp.where(won, ranks[r + 1], ranks[r])
    return jnp.concatenate(vals, axis=0), jnp.concatenate(ids, axis=0)


def _top16_rows(s):
    n_slabs = s.shape[0] // 8
    sub = lax.broadcasted_iota(I32, (8, s.shape[1]), 0)
    vals = [s[8 * r:8 * r + 8] for r in range(n_slabs)]
    keys = [sub + 8 * r for r in range(n_slabs)]
    for i, j in _sort_network(n_slabs):
        a, b, ka, kb = vals[i], vals[j], keys[i], keys[j]
        swap = (b > a) | ((b == a) & (kb < ka))
        vals[i], vals[j] = jnp.where(swap, b, a), jnp.where(swap, a, b)
        keys[i], keys[j] = jnp.where(swap, kb, ka), jnp.where(swap, ka, kb)
    return _pop_merge(vals, keys, PEER_TOPK, PEER_N_KEYS)


def _product_top16(v1, n1, v2, n2):
    k = PEER_TOPK
    a_low = lax.broadcasted_iota(I32, (8, v1.shape[1]), 0)
    lists, ranks = [], []
    for b in range(k):
        longest = k // (b + 1)
        lists.append(jnp.where(a_low < longest, v1[0:8] + v2[b:b + 1], -jnp.inf))
        ranks.append(a_low * k + b)
    singles = (v1[8:k] + v2[0:1], (a_low + 8) * k)
    best, code = _pop_merge(lists, ranks, k, k * k, singles)
    rank_a, rank_b = code >> 4, code & (PEER_TOPK - 1)
    bi = jnp.zeros_like(code)
    bj = jnp.zeros_like(code)
    for r in range(PEER_TOPK):
        bi = jnp.where(rank_a == r, n1[r:r + 1], bi)
        bj = jnp.where(rank_b == r, n2[r:r + 1], bj)
    return best, bi, bj


def _route_kernel(h_ref, wq_ref, sk_ref, i_ref, j_ref, w_ref,
                  q_s, v_s, n_s, it_s, jt_s, wt_s):
    tr = h_ref.shape[0]
    nsub = tr // LANES
    x = h_ref[...].astype(BF16)
    q_t = lax.dot_general(wq_ref[...], x, (((1,), (1,)), ((), ())),
                          preferred_element_type=F32)
    q_s[...] = q_t.astype(BF16)

    def half_body(hp, _):
        qs = q_s[pl.ds(pl.multiple_of(hp * PEER_HALF, PEER_HALF), PEER_HALF), :]
        s_t = jnp.dot(sk_ref[hp], qs, preferred_element_type=F32)
        for sub in range(nsub):
            sl = slice(sub * LANES, (sub + 1) * LANES)
            v, n = _top16_rows(s_t[:, sl])
            v_s[hp, :, sl] = v
            n_s[hp, :, sl] = n
        return 0

    lax.fori_loop(0, 2 * PEER_HEADS, half_body, 0)

    def head_body(hd, _):
        for sub in range(nsub):
            sl = slice(sub * LANES, (sub + 1) * LANES)
            v1, v2 = v_s[2 * hd, :, sl], v_s[2 * hd + 1, :, sl]
            n1, n2 = n_s[2 * hd, :, sl], n_s[2 * hd + 1, :, sl]
            best, bi, bj = _product_top16(v1, n1, v2, n2)
            e = jnp.exp(best - best[0:1])
            gate = e / jnp.sum(e, axis=0, keepdims=True)
            rows = pl.ds(pl.multiple_of(hd * PEER_TOPK, PEER_TOPK), PEER_TOPK)
            it_s[rows, sl] = bi
            jt_s[rows, sl] = bj
            wt_s[rows, sl] = gate
        return 0

    lax.fori_loop(0, PEER_HEADS, head_body, 0)
    i_ref[...] = it_s[...].T
    j_ref[...] = jt_s[...].T
    w_ref[...] = wt_s[...].T


def _peer_route(h2, wq_t, sk):
    t, d = h2.shape
    tr = min(t, 1024)
    assert t % tr == 0 and tr % LANES == 0, (t, tr)
    nq = wq_t.shape[0]
    hk = PEER_HEADS * PEER_TOPK
    blk = pl.BlockSpec((tr, hk), lambda i: (i, 0))
    return pl.pallas_call(
        _route_kernel,
        grid=(t // tr,),
        in_specs=[
            pl.BlockSpec((tr, d), lambda i: (i, 0)),
            pl.BlockSpec((nq, d), lambda i: (0, 0)),
            pl.BlockSpec(sk.shape, lambda i: (0, 0, 0)),
        ],
        out_specs=[blk, blk, blk],
        out_shape=[jax.ShapeDtypeStruct((t, hk), I32), jax.ShapeDtypeStruct((t, hk), I32),
                   jax.ShapeDtypeStruct((t, hk), F32)],
        scratch_shapes=[
            pltpu.VMEM((nq, tr), BF16),
            pltpu.VMEM((2 * PEER_HEADS, PEER_TOPK, tr), F32),
            pltpu.VMEM((2 * PEER_HEADS, PEER_TOPK, tr), I32),
            pltpu.VMEM((hk, tr), I32),
            pltpu.VMEM((hk, tr), I32),
            pltpu.VMEM((hk, tr), F32),
        ],
        name="peer_route",
        compiler_params=_cparams(("arbitrary",)),
    )(h2, wq_t, sk)


def _expert_kernel(h_ref, i_ref, j_ref, w_ref, u_ref, v_ref, g_ref, b_ref, o_ref,
                   xb_s, gate_s, acc_s, *, alpha, n_chunks, sub_e):
    c = pl.program_id(1)
    tm = h_ref.shape[0]
    half = tm // 2
    ec = u_ref.shape[0]
    hk = i_ref.shape[1]

    @pl.when(c == 0)
    def _():
        xb_s[...] = h_ref[...].astype(BF16)
        acc_s[...] = jnp.zeros_like(acc_s)
        key = lax.broadcasted_iota(I32, (PEER_N_KEYS, hk), 0)
        zeros = jnp.zeros((PEER_N_KEYS, hk), BF16)

        def group_body(p, _):
            k0 = pl.multiple_of(p * G_PAIRS, G_PAIRS)
            k1 = pl.multiple_of(half + k0, G_PAIRS)
            i_blk = (i_ref[pl.ds(k0, G_PAIRS), :], i_ref[pl.ds(k1, G_PAIRS), :])
            j_blk = (j_ref[pl.ds(k0, G_PAIRS), :], j_ref[pl.ds(k1, G_PAIRS), :])
            w_blk = (w_ref[pl.ds(k0, G_PAIRS), :], w_ref[pl.ds(k1, G_PAIRS), :])
            for pair in range(G_PAIRS):
                lhs, rhs = [], []
                for side in range(2):
                    it = jnp.broadcast_to(i_blk[side][pair:pair + 1, :], (PEER_N_KEYS, hk))
                    jt = jnp.broadcast_to(j_blk[side][pair:pair + 1, :], (PEER_N_KEYS, hk))
                    wt = jnp.broadcast_to(w_blk[side][pair:pair + 1, :], (PEER_N_KEYS, hk))
                    lhs.append(jnp.where(key == it, 1.0, 0.0).astype(BF16))
                    rhs.append(jnp.where(key == jt, wt, 0.0).astype(BF16))
                x_mat = jnp.concatenate(lhs, axis=1)
                y_mat = jnp.concatenate(
                    [jnp.concatenate([rhs[0], zeros], axis=1),
                     jnp.concatenate([zeros, rhs[1]], axis=1)], axis=0)
                gp = lax.dot_general(x_mat, y_mat, (((1,), (1,)), ((), ())),
                                     preferred_element_type=F32)
                base = pl.multiple_of((k0 + pair) * G_PITCH, 8)
                gate_s[pl.ds(base, PEER_N_KEYS), :] = _pack_bf16_pair(gp[:, :LANES], gp[:, LANES:])
            return 0

        lax.fori_loop(0, half // G_PAIRS, group_body, 0)

    i_base = c * (ec // PEER_N_KEYS)
    per_sub = sub_e // PEER_N_KEYS
    for sc in range(ec // sub_e):
        a = lax.dot_general(xb_s[...], u_ref[sc * sub_e:(sc + 1) * sub_e, :], (((1,), (1,)), ((), ())),
                            preferred_element_type=F32)
        gate = jnp.concatenate(
            [jnp.concatenate(_unpack_bf16_pair(
                gate_s[pl.ds(i_base + sc * per_sub + r, half, stride=G_PITCH), :]), axis=0)
             for r in range(per_sub)], axis=1)
        act = 0.5 * a * (1.0 + lax.erf(a * (2.0 ** -0.5)))
        hm = (act * gate).astype(BF16)
        acc_s[...] += jnp.dot(hm, v_ref[sc * sub_e:(sc + 1) * sub_e, :],
                              preferred_element_type=F32)

    @pl.when(c == n_chunks - 1)
    def _():
        o_ref[...] = _layernorm(alpha * h_ref[...] + acc_s[...], g_ref[...], b_ref[...])


def _peer_experts(h2, ri, rj, rw, u_tab, v_tab, g, b, alpha):
    t, d = h2.shape
    n_exp = v_tab.shape[0]
    tm = min(t, 512)
    ec = 2048
    assert t % tm == 0 and tm % (2 * G_PAIRS) == 0 and n_exp % ec == 0, (t, n_exp)
    nc = n_exp // ec
    hk = ri.shape[1]
    tok = pl.BlockSpec((tm, d), lambda i, c: (i, 0))
    rt = pl.BlockSpec((tm, hk), lambda i, c: (i, 0))
    vec = pl.BlockSpec((1, d), lambda i, c: (0, 0))
    return pl.pallas_call(
        functools.partial(_expert_kernel, alpha=alpha, n_chunks=nc, sub_e=1024),
        grid=(t // tm, nc),
        in_specs=[tok, rt, rt, rt,
                  pl.BlockSpec((ec, d), lambda i, c: (c, 0)),
                  pl.BlockSpec((ec, d), lambda i, c: (c, 0)),
                  vec, vec],
        out_specs=tok,
        out_shape=jax.ShapeDtypeStruct((t, d), F32),
        scratch_shapes=[
            pltpu.VMEM((tm, d), BF16),
            pltpu.VMEM((tm // 2 * G_PITCH, LANES), jnp.uint32),
            pltpu.VMEM((tm, d), F32),
        ],
        name="peer_experts",
        compiler_params=_cparams(("arbitrary", "arbitrary")),
    )(h2, ri, rj, rw, u_tab, v_tab, g, b)


def _trunk(x, conv_prev, k_past, v_past, q_pos0, wts):
    bsz, seq, d = x.shape
    t = bsz * seq
    n_a = wts["conv_w_in"].shape[0]
    depth = n_a + wts["sb_w_q"].shape[0]
    alpha = (2.0 * depth) ** 0.25
    h = x
    new_conv = []
    k_new = v_new = k_all = v_all = None
    for layer in range(depth):
        g0, b0 = wts["ln_g"][layer, 0][None], wts["ln_b"][layer, 0][None]
        g1, b1 = wts["ln_g"][layer, 1][None], wts["ln_b"][layer, 1][None]
        if layer < n_a:
            h, st = _conv_layer(h, conv_prev[layer], wts["conv_w_in"][layer],
                                wts["conv_w_dw"][layer], wts["conv_w_out"][layer], g0, b0, alpha)
            new_conv.append(st)
        else:
            h2 = h.reshape(t, d)
            if layer == n_a:
                k2, v2, kb, vb = _kv_proj(h2, wts["kv_w"])
                k_new = k2.reshape(bsz, seq, d // SB_HEAD_DIM, SB_HEAD_DIM)
                v_new = v2.reshape(bsz, seq, d // SB_HEAD_DIM, SB_HEAD_DIM)
                k_all, v_all = kb.reshape(bsz, seq, d), vb.reshape(bsz, seq, d)
                if k_past is not None:
                    past = k_past.shape[1]
                    k_all = jnp.concatenate([k_past.reshape(bsz, past, d).astype(BF16), k_all], axis=1)
                    v_all = jnp.concatenate([v_past.reshape(bsz, past, d).astype(BF16), v_all], axis=1)
                pad = (-k_all.shape[1]) % 256
                if pad:
                    k_all = jnp.pad(k_all, ((0, 0), (0, pad), (0, 0)))
                    v_all = jnp.pad(v_all, ((0, 0), (0, pad), (0, 0)))
            j = layer - n_a
            q = _q_proj(h2, wts["sb_w_q"][j]).reshape(bsz, seq, d)
            att = _sb_attention(q, k_all, v_all, q_pos0)
            h = _oproj_ln(att.reshape(t, d), h2, wts["sb_w_o"][j], g0, b0, alpha).reshape(bsz, seq, d)
        h2 = h.reshape(t, d)
        ri, rj, rw = _peer_route(h2, wts["peer_wq_t"][layer], wts["peer_sk"][layer])
        h = _peer_experts(h2, ri, rj, rw, wts["peer_u"][layer], wts["peer_v"][layer],
                          g1, b1, alpha).reshape(bsz, seq, d)
    return h, jnp.stack(new_conv, axis=0), k_new, v_new


def kernel(x_prompt, x_sample, state_conv, cache_k, cache_v, conv_w_in, conv_w_dw, conv_w_out,
           sb_w_q, sb_w_o, kv_w_k, kv_w_v, peer_w_q, peer_subkeys, peer_u, peer_v, ln_g, ln_b):
    depth = peer_w_q.shape[0]
    wts = {
        "conv_w_in": conv_w_in.astype(BF16),
        "conv_w_dw": conv_w_dw,
        "conv_w_out": conv_w_out.astype(BF16),
        "sb_w_q": sb_w_q.astype(BF16),
        "sb_w_o": sb_w_o.astype(BF16),
        "kv_w": jnp.concatenate([kv_w_k, kv_w_v], axis=1).astype(BF16),
        "peer_wq_t": jnp.swapaxes(peer_w_q, 1, 2).astype(BF16),
        "peer_sk": peer_subkeys.reshape(depth, 2 * PEER_HEADS, PEER_N_KEYS, PEER_HALF).astype(BF16),
        "peer_u": peer_u.astype(BF16),
        "peer_v": peer_v.astype(BF16),
        "ln_g": ln_g,
        "ln_b": ln_b,
    }
    n_a = conv_w_in.shape[0]
    zero_conv = jnp.zeros((n_a, x_prompt.shape[0], state_conv.shape[2], x_prompt.shape[2]), x_prompt.dtype)
    y_p, conv_p, k_p, v_p = _trunk(x_prompt, zero_conv, None, None, 0, wts)
    y_s, conv_s, k_s, v_s = _trunk(x_sample, state_conv, cache_k, cache_v, cache_k.shape[1], wts)
    return (y_p, y_s, conv_p, k_p, v_p, conv_s, k_s, v_s)
```

```python
import functools
import math

import jax
import jax.numpy as jnp
from jax import lax
from jax.experimental import pallas as pl
from jax.experimental.pallas import tpu as pltpu

F32 = jnp.float32
BF16 = jnp.bfloat16
I32 = jnp.int32

LN_EPS = 1e-5
SB_HEAD_DIM = 64
PEER_HEADS = 8
PEER_N_KEYS = 128
PEER_TOPK = 16
PEER_HALF = 128
LANES = 128
G_PITCH = 136
VMEM_LIMIT = 56 * 1024 * 1024
EXP_UNDERFLOW = -104.0
G_PAIRS = 64
SB_HEADS_PER_STEP = 2

ROW_TILE = 512
SB_Q_TILE = 256
SB_K_BLOCK = 256
ROUTE_TILE = 1024
EXPERT_TILE = 512
EXPERT_CHUNK = 2048
EXPERT_SUB = 1024


def _cparams(sem):
    return pltpu.CompilerParams(dimension_semantics=sem, vmem_limit_bytes=VMEM_LIMIT)


def _pack_bf16_pair(lo, hi):
    lo_bits = lax.bitcast_convert_type(lo.astype(BF16).astype(F32), jnp.uint32) >> 16
    hi_bits = lax.bitcast_convert_type(hi.astype(BF16).astype(F32), jnp.uint32) & jnp.uint32(0xFFFF0000)
    return lo_bits | hi_bits


def _unpack_bf16_pair(packed):
    lo = lax.bitcast_convert_type(packed << 16, F32)
    hi = lax.bitcast_convert_type(packed & jnp.uint32(0xFFFF0000), F32)
    return lo, hi


def _layernorm(v, g, b):
    mu = jnp.mean(v, axis=-1, keepdims=True)
    d = v - mu
    var = jnp.mean(d * d, axis=-1, keepdims=True)
    return d * lax.rsqrt(var + LN_EPS) * g + b


def _conv_kernel(x_ref, prev_ref, win_ref, wdw_ref, wout_ref, g_ref, b_ref,
                 o_ref, st_ref, carry_ref, *, alpha, n_seq_tiles):
    s = pl.program_id(1)
    x = x_ref[0]
    tm, d = x.shape
    proj = jnp.dot(x.astype(BF16), win_ref[...], preferred_element_type=F32)
    b_gate = proj[:, :d]
    u = proj[:, d:2 * d] * proj[:, 2 * d:]

    @pl.when(s == 0)
    def _():
        carry_ref[0:2, :] = prev_ref[0]

    p0 = carry_ref[0:1, :]
    p1 = carry_ref[1:2, :]
    row = lax.broadcasted_iota(I32, (tm, d), 0)
    um1 = jnp.where(row == 0, p1, pltpu.roll(u, 1, 0))
    um2 = jnp.where(row == 0, p0, jnp.where(row == 1, p1, pltpu.roll(u, 2, 0)))
    acc = wdw_ref[0:1, :] * um2 + wdw_ref[1:2, :] * um1 + wdw_ref[2:3, :] * u
    carry_ref[0:2, :] = u[tm - 2:tm, :]
    y = jnp.dot((b_gate * acc).astype(BF16), wout_ref[...], preferred_element_type=F32)
    o_ref[0] = _layernorm(alpha * x + y, g_ref[...], b_ref[...])

    @pl.when(s == n_seq_tiles - 1)
    def _():
        st_ref[0] = u[tm - 2:tm, :]


def _conv_layer(h, prev, w_in, w_dw, w_out, g, b, alpha):
    bsz, seq, d = h.shape
    tm = min(seq, ROW_TILE)
    ns = seq // tm
    return pl.pallas_call(
        functools.partial(_conv_kernel, alpha=alpha, n_seq_tiles=ns),
        grid=(bsz, ns),
        in_specs=[
            pl.BlockSpec((1, tm, d), lambda i, j: (i, j, 0)),
            pl.BlockSpec((1, 2, d), lambda i, j: (i, 0, 0)),
            pl.BlockSpec((d, 3 * d), lambda i, j: (0, 0)),
            pl.BlockSpec((3, d), lambda i, j: (0, 0)),
            pl.BlockSpec((d, d), lambda i, j: (0, 0)),
            pl.BlockSpec((1, d), lambda i, j: (0, 0)),
            pl.BlockSpec((1, d), lambda i, j: (0, 0)),
        ],
        out_specs=[
            pl.BlockSpec((1, tm, d), lambda i, j: (i, j, 0)),
            pl.BlockSpec((1, 2, d), lambda i, j: (i, 0, 0)),
        ],
        out_shape=[
            jax.ShapeDtypeStruct((bsz, seq, d), F32),
            jax.ShapeDtypeStruct((bsz, 2, d), F32),
        ],
        scratch_shapes=[pltpu.VMEM((8, d), F32)],
        name="conv_mixer",
        compiler_params=_cparams(("arbitrary", "arbitrary")),
    )(h, prev, w_in, w_dw, w_out, g, b)


def _kv_kernel(x_ref, w_ref, k_ref, v_ref, kb_ref, vb_ref):
    d = x_ref.shape[1]
    y = jnp.dot(x_ref[...].astype(BF16), w_ref[...], preferred_element_type=F32)
    k = y[:, :d]
    v = y[:, d:]
    k_ref[...] = k
    v_ref[...] = v
    kb_ref[...] = k.astype(BF16)
    vb_ref[...] = v.astype(BF16)


def _kv_proj(h2, w_kv):
    t, d = h2.shape
    tm = min(t, ROW_TILE)
    blk = pl.BlockSpec((tm, d), lambda i: (i, 0))
    return pl.pallas_call(
        _kv_kernel,
        grid=(t // tm,),
        in_specs=[blk, pl.BlockSpec((d, 2 * d), lambda i: (0, 0))],
        out_specs=[blk, blk, blk, blk],
        out_shape=[
            jax.ShapeDtypeStruct((t, d), F32), jax.ShapeDtypeStruct((t, d), F32),
            jax.ShapeDtypeStruct((t, d), BF16), jax.ShapeDtypeStruct((t, d), BF16),
        ],
        name="kv_proj",
        compiler_params=_cparams(("arbitrary",)),
    )(h2, w_kv)


def _q_kernel(x_ref, w_ref, o_ref, *, scale):
    o_ref[...] = (jnp.dot(x_ref[...].astype(BF16), w_ref[...],
                          preferred_element_type=F32) * scale).astype(o_ref.dtype)


def _q_proj(h2, w_q):
    t, d = h2.shape
    tm = min(t, ROW_TILE)
    blk = pl.BlockSpec((tm, d), lambda i: (i, 0))
    scale = SB_HEAD_DIM ** -0.5
    assert math.frexp(scale)[0] == 0.5, scale
    return pl.pallas_call(
        functools.partial(_q_kernel, scale=scale),
        grid=(t // tm,),
        in_specs=[blk, pl.BlockSpec((d, d), lambda i: (0, 0))],
        out_specs=blk,
        out_shape=jax.ShapeDtypeStruct((t, d), BF16),
        name="q_proj",
        compiler_params=_cparams(("arbitrary",)),
    )(h2, w_q)


def _oproj_ln_kernel(a_ref, h_ref, w_ref, g_ref, b_ref, o_ref, *, alpha):
    y = jnp.dot(a_ref[...], w_ref[...], preferred_element_type=F32)
    o_ref[...] = _layernorm(alpha * h_ref[...] + y, g_ref[...], b_ref[...])


def _oproj_ln(a2, h2, w_o, g, b, alpha):
    t, d = h2.shape
    tm = min(t, ROW_TILE)
    blk = pl.BlockSpec((tm, d), lambda i: (i, 0))
    vec = pl.BlockSpec((1, d), lambda i: (0, 0))
    return pl.pallas_call(
        functools.partial(_oproj_ln_kernel, alpha=alpha),
        grid=(t // tm,),
        in_specs=[blk, blk, pl.BlockSpec((d, d), lambda i: (0, 0)), vec, vec],
        out_specs=blk,
        out_shape=jax.ShapeDtypeStruct((t, d), F32),
        name="oproj_ln",
        compiler_params=_cparams(("arbitrary",)),
    )(a2, h2, w_o, g, b)


def _sb_kernel(q_ref, k_ref, v_ref, tri_ref, o_ref, *, q_pos0, tq, tk, n_kblocks, width):
    for group in range(q_ref.shape[2] // width):
        _sb_head_group(q_ref, k_ref, v_ref, tri_ref, o_ref, slice(group * width, (group + 1) * width),
                       q_pos0=q_pos0, tq=tq, tk=tk, n_kblocks=n_kblocks)


def _sb_head_group(q_ref, k_ref, v_ref, tri_ref, o_ref, lanes, *, q_pos0, tq, tk, n_kblocks):
    qi = pl.program_id(1)
    q_all = q_ref[0, :, lanes]
    width = q_all.shape[1]
    n_heads = width // SB_HEAD_DIM
    head_of_lane = lax.broadcasted_iota(I32, (1, width), 1) // SB_HEAD_DIM
    zero = jnp.zeros_like(q_all)
    q_heads = [jnp.where(head_of_lane == hd, q_all, zero) for hd in range(n_heads)]
    q_start = q_pos0 + qi * tq
    qpos = q_start + lax.broadcasted_iota(I32, (tq, 1), 0)
    nblk = jnp.minimum(n_kblocks, (q_start + tq - 1 + tk - 1) // tk)

    def cond(carry):
        return jnp.logical_and(carry[0] < nblk, carry[1] > 0)

    def body(carry):
        j = carry[0]
        kb = nblk - 1 - j
        start = pl.multiple_of(kb * tk, tk)
        kblk = k_ref[0, pl.ds(start, tk), lanes]
        vblk = v_ref[0, pl.ds(start, tk), lanes]
        kpos = start + lax.broadcasted_iota(I32, (1, tk), 1)
        causal = kpos < qpos
        out = []
        for hd in range(n_heads):
            acc, run = carry[2 + 2 * hd], carry[3 + 2 * hd]
            z = lax.dot_general(q_heads[hd], kblk, (((1,), (1,)), ((), ())),
                                preferred_element_type=F32)
            log_sig = jnp.minimum(z, 0.0) - jnp.log(1.0 + jnp.exp(-jnp.abs(z)))
            log_not = jnp.where(causal, log_sig - z, 0.0)
            suffix = jnp.dot(log_not.astype(BF16), tri_ref[...],
                             preferred_element_type=F32) + run
            a = jnp.where(causal, jnp.exp(log_sig + suffix), 0.0)
            acc = acc + jnp.dot(a.astype(BF16), vblk, preferred_element_type=F32)
            run = run + jnp.sum(log_not, axis=1, keepdims=True)
            out += [acc, run]
        live = (jnp.max(functools.reduce(jnp.maximum, out[1::2])) > EXP_UNDERFLOW).astype(I32)
        return (j + 1, live, *out)

    init = (jnp.int32(0), jnp.int32(1)) + (jnp.zeros((tq, width), F32), jnp.zeros((tq, 1), F32)) * n_heads
    res = lax.while_loop(cond, body, init)
    result = res[2]
    for hd in range(1, n_heads):
        result = jnp.where(head_of_lane == hd, res[2 + 2 * hd], result)
    o_ref[0, :, lanes] = result.astype(o_ref.dtype)


def _sb_attention(q, k_all, v_all, q_pos0):
    bsz, sq, d = q.shape
    tq = min(sq, SB_Q_TILE)
    tk = SB_K_BLOCK
    sk = k_all.shape[1]
    n_kblocks = sk // tk
    tri = (lax.broadcasted_iota(I32, (tk, tk), 0) >
           lax.broadcasted_iota(I32, (tk, tk), 1)).astype(BF16)
    width = SB_HEADS_PER_STEP * SB_HEAD_DIM
    assert d % width == 0 and width % LANES == 0, (d, width)
    return pl.pallas_call(
        functools.partial(_sb_kernel, q_pos0=q_pos0, tq=tq, tk=tk, n_kblocks=n_kblocks, width=width),
        grid=(bsz, sq // tq),
        in_specs=[
            pl.BlockSpec((1, tq, d), lambda b, i: (b, i, 0)),
            pl.BlockSpec((1, sk, d), lambda b, i: (b, 0, 0)),
            pl.BlockSpec((1, sk, d), lambda b, i: (b, 0, 0)),
            pl.BlockSpec((tk, tk), lambda b, i: (0, 0)),
        ],
        out_specs=pl.BlockSpec((1, tq, d), lambda b, i: (b, i, 0)),
        out_shape=jax.ShapeDtypeStruct((bsz, sq, d), BF16),
        name="sb_attention",
        compiler_params=_cparams(("arbitrary", "arbitrary")),
    )(q, k_all, v_all, tri)


def _sort_network(n):
    pairs = []
    p = 1
    while p < n:
        k = p
        while k >= 1:
            for j in range(k % p, n - k, 2 * k):
                for i in range(min(k, n - j - k)):
                    if (i + j) // (2 * p) == (i + j + k) // (2 * p):
                        pairs.append((i + j, i + j + k))
            k //= 2
        p *= 2
    return pairs


def _pop_merge(lists, ranks, steps, big, singles=None):
    lists, ranks = list(lists), list(ranks)
    vals, ids = [], []
    for step in range(steps):
        head, head_rank = lists[0], ranks[0]
        if singles is not None:
            head = jnp.concatenate([head, singles[0]], axis=0)
            head_rank = jnp.concatenate([head_rank, singles[1]], axis=0)
        m = jnp.max(head, axis=0, keepdims=True)
        sel = jnp.min(jnp.where(head == m, head_rank, big), axis=0, keepdims=True)
        if singles is not None:
            singles = (jnp.where(singles[1] == sel, -jnp.inf, singles[0]), singles[1])
        vals.append(m)
        ids.append(sel)
        won = ranks[0] == sel
        for r in range(min(steps - 1 - step, len(lists) - 1)):
            lists[r] = jnp.where(won, lists[r + 1], lists[r])
            ranks[r] = jnp.where(won, ranks[r + 1], ranks[r])
    return jnp.concatenate(vals, axis=0), jnp.concatenate(ids, axis=0)


def _top16_rows(s):
    n_slabs = s.shape[0] // 8
    sub = lax.broadcasted_iota(I32, (8, s.shape[1]), 0)
    vals = [s[8 * r:8 * r + 8] for r in range(n_slabs)]
    keys = [sub + 8 * r for r in range(n_slabs)]
    for i, j in _sort_network(n_slabs):
        a, b, ka, kb = vals[i], vals[j], keys[i], keys[j]
        swap = (b > a) | ((b == a) & (kb < ka))
        vals[i], vals[j] = jnp.where(swap, b, a), jnp.where(swap, a, b)
        keys[i], keys[j] = jnp.where(swap, kb, ka), jnp.where(swap, ka, kb)
    return _pop_merge(vals, keys, PEER_TOPK, PEER_N_KEYS)


def _product_top16(v1, n1, v2, n2):
    k = PEER_TOPK
    a_low = lax.broadcasted_iota(I32, (8, v1.shape[1]), 0)
    lists, ranks = [], []
    for b in range(k):
        longest = k // (b + 1)
        lists.append(jnp.where(a_low < longest, v1[0:8] + v2[b:b + 1], -jnp.inf))
        ranks.append(a_low * k + b)
    singles = (v1[8:k] + v2[0:1], (a_low + 8) * k)
    best, code = _pop_merge(lists, ranks, k, k * k, singles)
    rank_a, rank_b = code >> 4, code & (PEER_TOPK - 1)
    bi = jnp.zeros_like(code)
    bj = jnp.zeros_like(code)
    for r in range(PEER_TOPK):
        bi = jnp.where(rank_a == r, n1[r:r + 1], bi)
        bj = jnp.where(rank_b == r, n2[r:r + 1], bj)
    return best, bi, bj


def _route_kernel(h_ref, wq_ref, sk_ref, i_ref, j_ref, w_ref,
                  q_s, v_s, n_s, it_s, jt_s, wt_s):
    tr = h_ref.shape[0]
    nsub = tr // LANES
    x = h_ref[...].astype(BF16)
    q_t = lax.dot_general(wq_ref[...], x, (((1,), (1,)), ((), ())),
                          preferred_element_type=F32)
    q_s[...] = q_t.astype(BF16)

    def half_body(hp, _):
        qs = q_s[pl.ds(pl.multiple_of(hp * PEER_HALF, PEER_HALF), PEER_HALF), :]
        s_t = jnp.dot(sk_ref[hp], qs, preferred_element_type=F32)
        for sub in range(nsub):
            sl = slice(sub * LANES, (sub + 1) * LANES)
            v, n = _top16_rows(s_t[:, sl])
            v_s[hp, :, sl] = v
            n_s[hp, :, sl] = n
        return 0

    lax.fori_loop(0, 2 * PEER_HEADS, half_body, 0)

    def head_body(hd, _):
        for sub in range(nsub):
            sl = slice(sub * LANES, (sub + 1) * LANES)
            v1, v2 = v_s[2 * hd, :, sl], v_s[2 * hd + 1, :, sl]
            n1, n2 = n_s[2 * hd, :, sl], n_s[2 * hd + 1, :, sl]
            best, bi, bj = _product_top16(v1, n1, v2, n2)
            e = jnp.exp(best - best[0:1])
            gate = e / jnp.sum(e, axis=0, keepdims=True)
            rows = pl.ds(pl.multiple_of(hd * PEER_TOPK, PEER_TOPK), PEER_TOPK)
            it_s[rows, sl] = bi
            jt_s[rows, sl] = bj
            wt_s[rows, sl] = gate
        return 0

    lax.fori_loop(0, PEER_HEADS, head_body, 0)
    i_ref[...] = it_s[...].T
    j_ref[...] = jt_s[...].T
    w_ref[...] = wt_s[...].T


def _peer_route(h2, wq_t, sk):
    t, d = h2.shape
    tr = min(t, ROUTE_TILE)
    assert t % tr == 0 and tr % LANES == 0, (t, tr)
    nq = wq_t.shape[0]
    hk = PEER_HEADS * PEER_TOPK
    blk = pl.BlockSpec((tr, hk), lambda i: (i, 0))
    return pl.pallas_call(
        _route_kernel,
        grid=(t // tr,),
        in_specs=[
            pl.BlockSpec((tr, d), lambda i: (i, 0)),
            pl.BlockSpec((nq, d), lambda i: (0, 0)),
            pl.BlockSpec(sk.shape, lambda i: (0, 0, 0)),
        ],
        out_specs=[blk, blk, blk],
        out_shape=[jax.ShapeDtypeStruct((t, hk), I32), jax.ShapeDtypeStruct((t, hk), I32),
                   jax.ShapeDtypeStruct((t, hk), F32)],
        scratch_shapes=[
            pltpu.VMEM((nq, tr), BF16),
            pltpu.VMEM((2 * PEER_HEADS, PEER_TOPK, tr), F32),
            pltpu.VMEM((2 * PEER_HEADS, PEER_TOPK, tr), I32),
            pltpu.VMEM((hk, tr), I32),
            pltpu.VMEM((hk, tr), I32),
            pltpu.VMEM((hk, tr), F32),
        ],
        name="peer_route",
        compiler_params=_cparams(("arbitrary",)),
    )(h2, wq_t, sk)


def _expert_kernel(h_ref, i_ref, j_ref, w_ref, u_ref, v_ref, g_ref, b_ref, o_ref,
                   xb_s, gate_s, acc_s, *, alpha, n_chunks, sub_e):
    c = pl.program_id(1)
    tm = h_ref.shape[0]
    half = tm // 2
    ec = u_ref.shape[0]
    hk = i_ref.shape[1]

    @pl.when(c == 0)
    def _():
        xb_s[...] = h_ref[...].astype(BF16)
        acc_s[...] = jnp.zeros_like(acc_s)
        key = lax.broadcasted_iota(I32, (PEER_N_KEYS, hk), 0)
        zeros = jnp.zeros((PEER_N_KEYS, hk), BF16)

        def group_body(p, _):
            k0 = pl.multiple_of(p * G_PAIRS, G_PAIRS)
            k1 = pl.multiple_of(half + k0, G_PAIRS)
            i_blk = (i_ref[pl.ds(k0, G_PAIRS), :], i_ref[pl.ds(k1, G_PAIRS), :])
            j_blk = (j_ref[pl.ds(k0, G_PAIRS), :], j_ref[pl.ds(k1, G_PAIRS), :])
            w_blk = (w_ref[pl.ds(k0, G_PAIRS), :], w_ref[pl.ds(k1, G_PAIRS), :])
            for pair in range(G_PAIRS):
                lhs, rhs = [], []
                for side in range(2):
                    it = jnp.broadcast_to(i_blk[side][pair:pair + 1, :], (PEER_N_KEYS, hk))
                    jt = jnp.broadcast_to(j_blk[side][pair:pair + 1, :], (PEER_N_KEYS, hk))
                    wt = jnp.broadcast_to(w_blk[side][pair:pair + 1, :], (PEER_N_KEYS, hk))
                    lhs.append(jnp.where(key == it, 1.0, 0.0).astype(BF16))
                    rhs.append(jnp.where(key == jt, wt, 0.0).astype(BF16))
                x_mat = jnp.concatenate(lhs, axis=1)
                y_mat = jnp.concatenate(
                    [jnp.concatenate([rhs[0], zeros], axis=1),
                     jnp.concatenate([zeros, rhs[1]], axis=1)], axis=0)
                gp = lax.dot_general(x_mat, y_mat, (((1,), (1,)), ((), ())),
                                     preferred_element_type=F32)
                base = pl.multiple_of((k0 + pair) * G_PITCH, 8)
                gate_s[pl.ds(base, PEER_N_KEYS), :] = _pack_bf16_pair(gp[:, :LANES], gp[:, LANES:])
            return 0

        lax.fori_loop(0, half // G_PAIRS, group_body, 0)

    i_base = c * (ec // PEER_N_KEYS)
    per_sub = sub_e // PEER_N_KEYS
    for sc in range(ec // sub_e):
        a = lax.dot_general(xb_s[...], u_ref[sc * sub_e:(sc + 1) * sub_e, :], (((1,), (1,)), ((), ())),
                            preferred_element_type=F32)
        gate = jnp.concatenate(
            [jnp.concatenate(_unpack_bf16_pair(
                gate_s[pl.ds(i_base + sc * per_sub + r, half, stride=G_PITCH), :]), axis=0)
             for r in range(per_sub)], axis=1)
        act = 0.5 * a * (1.0 + lax.erf(a * (2.0 ** -0.5)))
        hm = (act * gate).astype(BF16)
        acc_s[...] += jnp.dot(hm, v_ref[sc * sub_e:(sc + 1) * sub_e, :],
                              preferred_element_type=F32)

    @pl.when(c == n_chunks - 1)
    def _():
        o_ref[...] = _layernorm(alpha * h_ref[...] + acc_s[...], g_ref[...], b_ref[...])


def _peer_experts(h2, ri, rj, rw, u_tab, v_tab, g, b, alpha):
    t, d = h2.shape
    n_exp = v_tab.shape[0]
    tm = min(t, EXPERT_TILE)
    ec = EXPERT_CHUNK
    assert t % tm == 0 and tm % (2 * G_PAIRS) == 0 and n_exp % ec == 0, (t, n_exp)
    nc = n_exp // ec
    hk = ri.shape[1]
    tok = pl.BlockSpec((tm, d), lambda i, c: (i, 0))
    rt = pl.BlockSpec((tm, hk), lambda i, c: (i, 0))
    vec = pl.BlockSpec((1, d), lambda i, c: (0, 0))
    return pl.pallas_call(
        functools.partial(_expert_kernel, alpha=alpha, n_chunks=nc, sub_e=EXPERT_SUB),
        grid=(t // tm, nc),
        in_specs=[tok, rt, rt, rt,
                  pl.BlockSpec((ec, d), lambda i, c: (c, 0)),
                  pl.BlockSpec((ec, d), lambda i, c: (c, 0)),
                  vec, vec],
        out_specs=tok,
        out_shape=jax.ShapeDtypeStruct((t, d), F32),
        scratch_shapes=[
            pltpu.VMEM((tm, d), BF16),
            pltpu.VMEM((tm // 2 * G_PITCH, LANES), jnp.uint32),
            pltpu.VMEM((tm, d), F32),
        ],
        name="peer_experts",
        compiler_params=_cparams(("arbitrary", "arbitrary")),
    )(h2, ri, rj, rw, u_tab, v_tab, g, b)


def _trunk(x, conv_prev, k_past, v_past, q_pos0, wts):
    bsz, seq, d = x.shape
    t = bsz * seq
    n_a = wts["conv_w_in"].shape[0]
    depth = n_a + wts["sb_w_q"].shape[0]
    alpha = (2.0 * depth) ** 0.25
    h = x
    new_conv = []
    k_new = v_new = k_all = v_all = None
    for layer in range(depth):
        g0, b0 = wts["ln_g"][layer, 0][None], wts["ln_b"][layer, 0][None]
        g1, b1 = wts["ln_g"][layer, 1][None], wts["ln_b"][layer, 1][None]
        if layer < n_a:
            h, st = _conv_layer(h, conv_prev[layer], wts["conv_w_in"][layer],
                                wts["conv_w_dw"][layer], wts["conv_w_out"][layer], g0, b0, alpha)
            new_conv.append(st)
        else:
            h2 = h.reshape(t, d)
            if layer == n_a:
                k2, v2, kb, vb = _kv_proj(h2, wts["kv_w"])
                k_new = k2.reshape(bsz, seq, d // SB_HEAD_DIM, SB_HEAD_DIM)
                v_new = v2.reshape(bsz, seq, d // SB_HEAD_DIM, SB_HEAD_DIM)
                k_all, v_all = kb.reshape(bsz, seq, d), vb.reshape(bsz, seq, d)
                if k_past is not None:
                    past = k_past.shape[1]
                    k_all = jnp.concatenate([k_past.reshape(bsz, past, d).astype(BF16), k_all], axis=1)
                    v_all = jnp.concatenate([v_past.reshape(bsz, past, d).astype(BF16), v_all], axis=1)
                pad = (-k_all.shape[1]) % SB_K_BLOCK
                if pad:
                    k_all = jnp.pad(k_all, ((0, 0), (0, pad), (0, 0)))
                    v_all = jnp.pad(v_all, ((0, 0), (0, pad), (0, 0)))
            j = layer - n_a
            q = _q_proj(h2, wts["sb_w_q"][j]).reshape(bsz, seq, d)
            att = _sb_attention(q, k_all, v_all, q_pos0)
            h = _oproj_ln(att.reshape(t, d), h2, wts["sb_w_o"][j], g0, b0, alpha).reshape(bsz, seq, d)
        h2 = h.reshape(t, d)
        ri, rj, rw = _peer_route(h2, wts["peer_wq_t"][layer], wts["peer_sk"][layer])
        h = _peer_experts(h2, ri, rj, rw, wts["peer_u"][layer], wts["peer_v"][layer],
                          g1, b1, alpha).reshape(bsz, seq, d)
    return h, jnp.stack(new_conv, axis=0), k_new, v_new


def kernel(x_prompt, x_sample, state_conv, cache_k, cache_v, conv_w_in, conv_w_dw, conv_w_out,
           sb_w_q, sb_w_o, kv_w_k, kv_w_v, peer_w_q, peer_subkeys, peer_u, peer_v, ln_g, ln_b):
    depth = peer_w_q.shape[0]
    wts = {
        "conv_w_in": conv_w_in.astype(BF16),
        "conv_w_dw": conv_w_dw,
        "conv_w_out": conv_w_out.astype(BF16),
        "sb_w_q": sb_w_q.astype(BF16),
        "sb_w_o": sb_w_o.astype(BF16),
        "kv_w": jnp.concatenate([kv_w_k, kv_w_v], axis=1).astype(BF16),
        "peer_wq_t": jnp.swapaxes(peer_w_q, 1, 2).astype(BF16),
        "peer_sk": peer_subkeys.reshape(depth, 2 * PEER_HEADS, PEER_N_KEYS, PEER_HALF).astype(BF16),
        "peer_u": peer_u.astype(BF16),
        "peer_v": peer_v.astype(BF16),
        "ln_g": ln_g,
        "ln_b": ln_b,
    }
    n_a = conv_w_in.shape[0]
    zero_conv = jnp.zeros((n_a, x_prompt.shape[0], state_conv.shape[2], x_prompt.shape[2]), x_prompt.dtype)
    y_p, conv_p, k_p, v_p = _trunk(x_prompt, zero_conv, None, None, 0, wts)
    y_s, conv_s, k_s, v_s = _trunk(x_sample, state_conv, cache_k, cache_v, cache_k.shape[1], wts)
    return (y_p, y_s, conv_p, k_p, v_p, conv_s, k_s, v_s)
```

```python
import functools
import math

import jax
import jax.numpy as jnp
from jax import lax
from jax.experimental import pallas as pl
from jax.experimental.pallas import tpu as pltpu

F32 = jnp.float32
BF16 = jnp.bfloat16
I32 = jnp.int32

LN_EPS = 1e-5
SB_HEAD_DIM = 64
PEER_HEADS = 8
PEER_N_KEYS = 128
PEER_TOPK = 16
PEER_HALF = 128
LANES = 128
G_PITCH = 136
VMEM_LIMIT = 56 * 1024 * 1024
EXP_UNDERFLOW = -104.0
G_PAIRS = 64
SB_HEADS_PER_STEP = 2

ROW_TILE = 512
SB_Q_TILE = 256
SB_K_BLOCK = 512
SB_KEY_ALIGN = 16
ROUTE_TILE = 1024
EXPERT_TILE = 512
EXPERT_CHUNK = 2048
EXPERT_SUB = 1024


def _cparams(sem):
    return pltpu.CompilerParams(dimension_semantics=sem, vmem_limit_bytes=VMEM_LIMIT)


def _pack_bf16_pair(lo, hi):
    lo_bits = lax.bitcast_convert_type(lo.astype(BF16).astype(F32), jnp.uint32) >> 16
    hi_bits = lax.bitcast_convert_type(hi.astype(BF16).astype(F32), jnp.uint32) & jnp.uint32(0xFFFF0000)
    return lo_bits | hi_bits


def _unpack_bf16_pair(packed):
    lo = lax.bitcast_convert_type(packed << 16, F32)
    hi = lax.bitcast_convert_type(packed & jnp.uint32(0xFFFF0000), F32)
    return lo, hi


def _layernorm(v, g, b):
    mu = jnp.mean(v, axis=-1, keepdims=True)
    d = v - mu
    var = jnp.mean(d * d, axis=-1, keepdims=True)
    return d * lax.rsqrt(var + LN_EPS) * g + b


def _conv_kernel(x_ref, prev_ref, win_ref, wdw_ref, wout_ref, g_ref, b_ref,
                 o_ref, st_ref, carry_ref, *, alpha, n_seq_tiles):
    s = pl.program_id(1)
    x = x_ref[0]
    tm, d = x.shape
    proj = jnp.dot(x.astype(BF16), win_ref[...], preferred_element_type=F32)
    b_gate = proj[:, :d]
    u = proj[:, d:2 * d] * proj[:, 2 * d:]

    @pl.when(s == 0)
    def _():
        carry_ref[0:2, :] = prev_ref[0]

    p0 = carry_ref[0:1, :]
    p1 = carry_ref[1:2, :]
    row = lax.broadcasted_iota(I32, (tm, d), 0)
    um1 = jnp.where(row == 0, p1, pltpu.roll(u, 1, 0))
    um2 = jnp.where(row == 0, p0, jnp.where(row == 1, p1, pltpu.roll(u, 2, 0)))
    acc = wdw_ref[0:1, :] * um2 + wdw_ref[1:2, :] * um1 + wdw_ref[2:3, :] * u
    carry_ref[0:2, :] = u[tm - 2:tm, :]
    y = jnp.dot((b_gate * acc).astype(BF16), wout_ref[...], preferred_element_type=F32)
    o_ref[0] = _layernorm(alpha * x + y, g_ref[...], b_ref[...])

    @pl.when(s == n_seq_tiles - 1)
    def _():
        st_ref[0] = u[tm - 2:tm, :]


def _conv_layer(h, prev, w_in, w_dw, w_out, g, b, alpha):
    bsz, seq, d = h.shape
    tm = min(seq, ROW_TILE)
    ns = seq // tm
    return pl.pallas_call(
        functools.partial(_conv_kernel, alpha=alpha, n_seq_tiles=ns),
        grid=(bsz, ns),
        in_specs=[
            pl.BlockSpec((1, tm, d), lambda i, j: (i, j, 0)),
            pl.BlockSpec((1, 2, d), lambda i, j: (i, 0, 0)),
            pl.BlockSpec((d, 3 * d), lambda i, j: (0, 0)),
            pl.BlockSpec((3, d), lambda i, j: (0, 0)),
            pl.BlockSpec((d, d), lambda i, j: (0, 0)),
            pl.BlockSpec((1, d), lambda i, j: (0, 0)),
            pl.BlockSpec((1, d), lambda i, j: (0, 0)),
        ],
        out_specs=[
            pl.BlockSpec((1, tm, d), lambda i, j: (i, j, 0)),
            pl.BlockSpec((1, 2, d), lambda i, j: (i, 0, 0)),
        ],
        out_shape=[
            jax.ShapeDtypeStruct((bsz, seq, d), F32),
            jax.ShapeDtypeStruct((bsz, 2, d), F32),
        ],
        scratch_shapes=[pltpu.VMEM((8, d), F32)],
        name="conv_mixer",
        compiler_params=_cparams(("arbitrary", "arbitrary")),
    )(h, prev, w_in, w_dw, w_out, g, b)


def _kv_kernel(x_ref, w_ref, k_ref, v_ref, kb_ref, vb_ref):
    d = x_ref.shape[1]
    y = jnp.dot(x_ref[...].astype(BF16), w_ref[...], preferred_element_type=F32)
    k = y[:, :d]
    v = y[:, d:]
    k_ref[...] = k
    v_ref[...] = v
    kb_ref[...] = k.astype(BF16)
    vb_ref[...] = v.astype(BF16)


def _kv_proj(h2, w_kv):
    t, d = h2.shape
    tm = min(t, ROW_TILE)
    blk = pl.BlockSpec((tm, d), lambda i: (i, 0))
    return pl.pallas_call(
        _kv_kernel,
        grid=(t // tm,),
        in_specs=[blk, pl.BlockSpec((d, 2 * d), lambda i: (0, 0))],
        out_specs=[blk, blk, blk, blk],
        out_shape=[
            jax.ShapeDtypeStruct((t, d), F32), jax.ShapeDtypeStruct((t, d), F32),
            jax.ShapeDtypeStruct((t, d), BF16), jax.ShapeDtypeStruct((t, d), BF16),
        ],
        name="kv_proj",
        compiler_params=_cparams(("arbitrary",)),
    )(h2, w_kv)


def _q_kernel(x_ref, w_ref, o_ref, *, scale):
    o_ref[...] = (jnp.dot(x_ref[...].astype(BF16), w_ref[...],
                          preferred_element_type=F32) * scale).astype(o_ref.dtype)


def _q_proj(h2, w_q):
    t, d = h2.shape
    tm = min(t, ROW_TILE)
    blk = pl.BlockSpec((tm, d), lambda i: (i, 0))
    scale = SB_HEAD_DIM ** -0.5
    assert math.frexp(scale)[0] == 0.5, scale
    return pl.pallas_call(
        functools.partial(_q_kernel, scale=scale),
        grid=(t // tm,),
        in_specs=[blk, pl.BlockSpec((d, d), lambda i: (0, 0))],
        out_specs=blk,
        out_shape=jax.ShapeDtypeStruct((t, d), BF16),
        name="q_proj",
        compiler_params=_cparams(("arbitrary",)),
    )(h2, w_q)


def _oproj_ln_kernel(a_ref, h_ref, w_ref, g_ref, b_ref, o_ref, *, alpha):
    y = jnp.dot(a_ref[...], w_ref[...], preferred_element_type=F32)
    o_ref[...] = _layernorm(alpha * h_ref[...] + y, g_ref[...], b_ref[...])


def _oproj_ln(a2, h2, w_o, g, b, alpha):
    t, d = h2.shape
    tm = min(t, ROW_TILE)
    blk = pl.BlockSpec((tm, d), lambda i: (i, 0))
    vec = pl.BlockSpec((1, d), lambda i: (0, 0))
    return pl.pallas_call(
        functools.partial(_oproj_ln_kernel, alpha=alpha),
        grid=(t // tm,),
        in_specs=[blk, blk, pl.BlockSpec((d, d), lambda i: (0, 0)), vec, vec],
        out_specs=blk,
        out_shape=jax.ShapeDtypeStruct((t, d), F32),
        name="oproj_ln",
        compiler_params=_cparams(("arbitrary",)),
    )(a2, h2, w_o, g, b)


def _sb_kernel(q_ref, k_ref, v_ref, tri_ref, o_ref, *, q_pos0, tq, tk, width):
    for group in range(q_ref.shape[2] // width):
        _sb_head_group(q_ref, k_ref, v_ref, tri_ref, o_ref, slice(group * width, (group + 1) * width),
                       q_pos0=q_pos0, tq=tq, tk=tk)


def _sb_head_group(q_ref, k_ref, v_ref, tri_ref, o_ref, lanes, *, q_pos0, tq, tk):
    qi = pl.program_id(1)
    q_all = q_ref[0, :, lanes]
    width = q_all.shape[1]
    n_heads = width // SB_HEAD_DIM
    head_of_lane = lax.broadcasted_iota(I32, (1, width), 1) // SB_HEAD_DIM
    zero = jnp.zeros_like(q_all)
    q_heads = [jnp.where(head_of_lane == hd, q_all, zero) for hd in range(n_heads)]
    q_start = q_pos0 + qi * tq
    q_end = q_start + tq
    qpos = q_start + lax.broadcasted_iota(I32, (tq, 1), 0)
    nblk = (q_end + tk - 1) // tk

    def cond(carry):
        return jnp.logical_and(carry[0] < nblk, carry[1] > 0)

    def body(carry):
        j = carry[0]
        hi = q_end - j * tk
        start = pl.multiple_of(jnp.maximum(hi - tk, 0), SB_KEY_ALIGN)
        kblk = k_ref[0, pl.ds(start, tk), lanes]
        vblk = v_ref[0, pl.ds(start, tk), lanes]
        kpos = start + lax.broadcasted_iota(I32, (1, tk), 1)
        causal = kpos < jnp.minimum(qpos, hi)
        out = []
        for hd in range(n_heads):
            acc, run = carry[2 + 2 * hd], carry[3 + 2 * hd]
            z = lax.dot_general(q_heads[hd], kblk, (((1,), (1,)), ((), ())),
                                preferred_element_type=F32)
            log_sig = jnp.minimum(z, 0.0) - jnp.log(1.0 + jnp.exp(-jnp.abs(z)))
            log_not = jnp.where(causal, log_sig - z, 0.0)
            suffix = jnp.dot(log_not.astype(BF16), tri_ref[...],
                             preferred_element_type=F32) + run
            a = jnp.where(causal, jnp.exp(log_sig + suffix), 0.0)
            acc = acc + jnp.dot(a.astype(BF16), vblk, preferred_element_type=F32)
            run = run + jnp.sum(log_not, axis=1, keepdims=True)
            out += [acc, run]
        live = (jnp.max(functools.reduce(jnp.maximum, out[1::2])) > EXP_UNDERFLOW).astype(I32)
        return (j + 1, live, *out)

    init = (jnp.int32(0), jnp.int32(1)) + (jnp.zeros((tq, width), F32), jnp.zeros((tq, 1), F32)) * n_heads
    res = lax.while_loop(cond, body, init)
    result = res[2]
    for hd in range(1, n_heads):
        result = jnp.where(head_of_lane == hd, res[2 + 2 * hd], result)
    o_ref[0, :, lanes] = result.astype(o_ref.dtype)


def _sb_attention(q, k_all, v_all, q_pos0):
    bsz, sq, d = q.shape
    tq = min(sq, SB_Q_TILE)
    tk = SB_K_BLOCK
    sk = k_all.shape[1]
    tri = (lax.broadcasted_iota(I32, (tk, tk), 0) >
           lax.broadcasted_iota(I32, (tk, tk), 1)).astype(BF16)
    width = SB_HEADS_PER_STEP * SB_HEAD_DIM
    assert d % width == 0 and width % LANES == 0, (d, width)
    assert sk == q_pos0 + sq and sk >= tk and q_pos0 % SB_KEY_ALIGN == 0 and tq % SB_KEY_ALIGN == 0, (sk, q_pos0, sq)
    return pl.pallas_call(
        functools.partial(_sb_kernel, q_pos0=q_pos0, tq=tq, tk=tk, width=width),
        grid=(bsz, sq // tq),
        in_specs=[
            pl.BlockSpec((1, tq, d), lambda b, i: (b, i, 0)),
            pl.BlockSpec((1, sk, d), lambda b, i: (b, 0, 0)),
            pl.BlockSpec((1, sk, d), lambda b, i: (b, 0, 0)),
            pl.BlockSpec((tk, tk), lambda b, i: (0, 0)),
        ],
        out_specs=pl.BlockSpec((1, tq, d), lambda b, i: (b, i, 0)),
        out_shape=jax.ShapeDtypeStruct((bsz, sq, d), BF16),
        name="sb_attention",
        compiler_params=_cparams(("arbitrary", "arbitrary")),
    )(q, k_all, v_all, tri)


def _sort_network(n):
    pairs = []
    p = 1
    while p < n:
        k = p
        while k >= 1:
            for j in range(k % p, n - k, 2 * k):
                for i in range(min(k, n - j - k)):
                    if (i + j) // (2 * p) == (i + j + k) // (2 * p):
                        pairs.append((i + j, i + j + k))
            k //= 2
        p *= 2
    return pairs


def _pop_merge(lists, ranks, steps, big, singles=None):
    lists, ranks = list(lists), list(ranks)
    vals, ids = [], []
    for step in range(steps):
        head, head_rank = lists[0], ranks[0]
        if singles is not None:
            head = jnp.concatenate([head, singles[0]], axis=0)
            head_rank = jnp.concatenate([head_rank, singles[1]], axis=0)
        m = jnp.max(head, axis=0, keepdims=True)
        sel = jnp.min(jnp.where(head == m, head_rank, big), axis=0, keepdims=True)
        if singles is not None:
            singles = (jnp.where(singles[1] == sel, -jnp.inf, singles[0]), singles[1])
        vals.append(m)
        ids.append(sel)
        won = ranks[0] == sel
        for r in range(min(steps - 1 - step, len(lists) - 1)):
            lists[r] = jnp.where(won, lists[r + 1], lists[r])
            ranks[r] = jnp.where(won, ranks[r + 1], ranks[r])
    return jnp.concatenate(vals, axis=0), jnp.concatenate(ids, axis=0)


def _top16_rows(s):
    n_slabs = s.shape[0] // 8
    sub = lax.broadcasted_iota(I32, (8, s.shape[1]), 0)
    vals = [s[8 * r:8 * r + 8] for r in range(n_slabs)]
    keys = [sub + 8 * r for r in range(n_slabs)]
    for i, j in _sort_network(n_slabs):
        a, b, ka, kb = vals[i], vals[j], keys[i], keys[j]
        swap = (b > a) | ((b == a) & (kb < ka))
        vals[i], vals[j] = jnp.where(swap, b, a), jnp.where(swap, a, b)
        keys[i], keys[j] = jnp.where(swap, kb, ka), jnp.where(swap, ka, kb)
    return _pop_merge(vals, keys, PEER_TOPK, PEER_N_KEYS)


def _product_top16(v1, n1, v2, n2):
    k = PEER_TOPK
    a_low = lax.broadcasted_iota(I32, (8, v1.shape[1]), 0)
    lists, ranks = [], []
    for b in range(k):
        longest = k // (b + 1)
        lists.append(jnp.where(a_low < longest, v1[0:8] + v2[b:b + 1], -jnp.inf))
        ranks.append(a_low * k + b)
    singles = (v1[8:k] + v2[0:1], (a_low + 8) * k)
    best, code = _pop_merge(lists, ranks, k, k * k, singles)
    rank_a, rank_b = code >> 4, code & (PEER_TOPK - 1)
    bi = jnp.zeros_like(code)
    bj = jnp.zeros_like(code)
    for r in range(PEER_TOPK):
        bi = jnp.where(rank_a == r, n1[r:r + 1], bi)
        bj = jnp.where(rank_b == r, n2[r:r + 1], bj)
    return best, bi, bj


def _route_kernel(h_ref, wq_ref, sk_ref, i_ref, j_ref, w_ref,
                  q_s, v_s, n_s, it_s, jt_s, wt_s):
    tr = h_ref.shape[0]
    nsub = tr // LANES
    x = h_ref[...].astype(BF16)
    q_t = lax.dot_general(wq_ref[...], x, (((1,), (1,)), ((), ())),
                          preferred_element_type=F32)
    q_s[...] = q_t.astype(BF16)

    def half_body(hp, _):
        qs = q_s[pl.ds(pl.multiple_of(hp * PEER_HALF, PEER_HALF), PEER_HALF), :]
        s_t = jnp.dot(sk_ref[hp], qs, preferred_element_type=F32)
        for sub in range(nsub):
            sl = slice(sub * LANES, (sub + 1) * LANES)
            v, n = _top16_rows(s_t[:, sl])
            v_s[hp, :, sl] = v
            n_s[hp, :, sl] = n
        return 0

    lax.fori_loop(0, 2 * PEER_HEADS, half_body, 0)

    def head_body(hd, _):
        for sub in range(nsub):
            sl = slice(sub * LANES, (sub + 1) * LANES)
            v1, v2 = v_s[2 * hd, :, sl], v_s[2 * hd + 1, :, sl]
            n1, n2 = n_s[2 * hd, :, sl], n_s[2 * hd + 1, :, sl]
            best, bi, bj = _product_top16(v1, n1, v2, n2)
            e = jnp.exp(best - best[0:1])
            gate = e / jnp.sum(e, axis=0, keepdims=True)
            rows = pl.ds(pl.multiple_of(hd * PEER_TOPK, PEER_TOPK), PEER_TOPK)
            it_s[rows, sl] = bi
            jt_s[rows, sl] = bj
            wt_s[rows, sl] = gate
        return 0

    lax.fori_loop(0, PEER_HEADS, head_body, 0)
    i_ref[...] = it_s[...].T
    j_ref[...] = jt_s[...].T
    w_ref[...] = wt_s[...].T


def _peer_route(h2, wq_t, sk):
    t, d = h2.shape
    tr = min(t, ROUTE_TILE)
    assert t % tr == 0 and tr % LANES == 0, (t, tr)
    nq = wq_t.shape[0]
    hk = PEER_HEADS * PEER_TOPK
    blk = pl.BlockSpec((tr, hk), lambda i: (i, 0))
    return pl.pallas_call(
        _route_kernel,
        grid=(t // tr,),
        in_specs=[
            pl.BlockSpec((tr, d), lambda i: (i, 0)),
            pl.BlockSpec((nq, d), lambda i: (0, 0)),
            pl.BlockSpec(sk.shape, lambda i: (0, 0, 0)),
        ],
        out_specs=[blk, blk, blk],
        out_shape=[jax.ShapeDtypeStruct((t, hk), I32), jax.ShapeDtypeStruct((t, hk), I32),
                   jax.ShapeDtypeStruct((t, hk), F32)],
        scratch_shapes=[
            pltpu.VMEM((nq, tr), BF16),
            pltpu.VMEM((2 * PEER_HEADS, PEER_TOPK, tr), F32),
            pltpu.VMEM((2 * PEER_HEADS, PEER_TOPK, tr), I32),
            pltpu.VMEM((hk, tr), I32),
            pltpu.VMEM((hk, tr), I32),
            pltpu.VMEM((hk, tr), F32),
        ],
        name="peer_route",
        compiler_params=_cparams(("arbitrary",)),
    )(h2, wq_t, sk)


def _expert_kernel(h_ref, i_ref, j_ref, w_ref, u_ref, v_ref, g_ref, b_ref, o_ref,
                   xb_s, gate_s, acc_s, *, alpha, n_chunks, sub_e):
    c = pl.program_id(1)
    tm = h_ref.shape[0]
    half = tm // 2
    ec = u_ref.shape[0]
    hk = i_ref.shape[1]

    @pl.when(c == 0)
    def _():
        xb_s[...] = h_ref[...].astype(BF16)
        acc_s[...] = jnp.zeros_like(acc_s)
        key = lax.broadcasted_iota(I32, (PEER_N_KEYS, hk), 0)
        zeros = jnp.zeros((PEER_N_KEYS, hk), BF16)

        def group_body(p, _):
            k0 = pl.multiple_of(p * G_PAIRS, G_PAIRS)
            k1 = pl.multiple_of(half + k0, G_PAIRS)
            i_blk = (i_ref[pl.ds(k0, G_PAIRS), :], i_ref[pl.ds(k1, G_PAIRS), :])
            j_blk = (j_ref[pl.ds(k0, G_PAIRS), :], j_ref[pl.ds(k1, G_PAIRS), :])
            w_blk = (w_ref[pl.ds(k0, G_PAIRS), :], w_ref[pl.ds(k1, G_PAIRS), :])
            for pair in range(G_PAIRS):
                lhs, rhs = [], []
                for side in range(2):
                    it = jnp.broadcast_to(i_blk[side][pair:pair + 1, :], (PEER_N_KEYS, hk))
                    jt = jnp.broadcast_to(j_blk[side][pair:pair + 1, :], (PEER_N_KEYS, hk))
                    wt = jnp.broadcast_to(w_blk[side][pair:pair + 1, :], (PEER_N_KEYS, hk))
                    lhs.append(jnp.where(key == it, 1.0, 0.0).astype(BF16))
                    rhs.append(jnp.where(key == jt, wt, 0.0).astype(BF16))
                x_mat = jnp.concatenate(lhs, axis=1)
                y_mat = jnp.concatenate(
                    [jnp.concatenate([rhs[0], zeros], axis=1),
                     jnp.concatenate([zeros, rhs[1]], axis=1)], axis=0)
                gp = lax.dot_general(x_mat, y_mat, (((1,), (1,)), ((), ())),
                                     preferred_element_type=F32)
                base = pl.multiple_of((k0 + pair) * G_PITCH, 8)
                gate_s[pl.ds(base, PEER_N_KEYS), :] = _pack_bf16_pair(gp[:, :LANES], gp[:, LANES:])
            return 0

        lax.fori_loop(0, half // G_PAIRS, group_body, 0)

    i_base = c * (ec // PEER_N_KEYS)
    per_sub = sub_e // PEER_N_KEYS
    for sc in range(ec // sub_e):
        a = lax.dot_general(xb_s[...], u_ref[sc * sub_e:(sc + 1) * sub_e, :], (((1,), (1,)), ((), ())),
                            preferred_element_type=F32)
        gate = jnp.concatenate(
            [jnp.concatenate(_unpack_bf16_pair(
                gate_s[pl.ds(i_base + sc * per_sub + r, half, stride=G_PITCH), :]), axis=0)
             for r in range(per_sub)], axis=1)
        act = 0.5 * a * (1.0 + lax.erf(a * (2.0 ** -0.5)))
        hm = (act * gate).astype(BF16)
        acc_s[...] += jnp.dot(hm, v_ref[sc * sub_e:(sc + 1) * sub_e, :],
                              preferred_element_type=F32)

    @pl.when(c == n_chunks - 1)
    def _():
        o_ref[...] = _layernorm(alpha * h_ref[...] + acc_s[...], g_ref[...], b_ref[...])


def _peer_experts(h2, ri, rj, rw, u_tab, v_tab, g, b, alpha):
    t, d = h2.shape
    n_exp = v_tab.shape[0]
    tm = min(t, EXPERT_TILE)
    ec = EXPERT_CHUNK
    assert t % tm == 0 and tm % (2 * G_PAIRS) == 0 and n_exp % ec == 0, (t, n_exp)
    nc = n_exp // ec
    hk = ri.shape[1]
    tok = pl.BlockSpec((tm, d), lambda i, c: (i, 0))
    rt = pl.BlockSpec((tm, hk), lambda i, c: (i, 0))
    vec = pl.BlockSpec((1, d), lambda i, c: (0, 0))
    return pl.pallas_call(
        functools.partial(_expert_kernel, alpha=alpha, n_chunks=nc, sub_e=EXPERT_SUB),
        grid=(t // tm, nc),
        in_specs=[tok, rt, rt, rt,
                  pl.BlockSpec((ec, d), lambda i, c: (c, 0)),
                  pl.BlockSpec((ec, d), lambda i, c: (c, 0)),
                  vec, vec],
        out_specs=tok,
        out_shape=jax.ShapeDtypeStruct((t, d), F32),
        scratch_shapes=[
            pltpu.VMEM((tm, d), BF16),
            pltpu.VMEM((tm // 2 * G_PITCH, LANES), jnp.uint32),
            pltpu.VMEM((tm, d), F32),
        ],
        name="peer_experts",
        compiler_params=_cparams(("arbitrary", "arbitrary")),
    )(h2, ri, rj, rw, u_tab, v_tab, g, b)


def _trunk(x, conv_prev, k_past, v_past, q_pos0, wts):
    bsz, seq, d = x.shape
    t = bsz * seq
    n_a = wts["conv_w_in"].shape[0]
    depth = n_a + wts["sb_w_q"].shape[0]
    alpha = (2.0 * depth) ** 0.25
    h = x
    new_conv = []
    k_new = v_new = k_all = v_all = None
    for layer in range(depth):
        g0, b0 = wts["ln_g"][layer, 0][None], wts["ln_b"][layer, 0][None]
        g1, b1 = wts["ln_g"][layer, 1][None], wts["ln_b"][layer, 1][None]
        if layer < n_a:
            h, st = _conv_layer(h, conv_prev[layer], wts["conv_w_in"][layer],
                                wts["conv_w_dw"][layer], wts["conv_w_out"][layer], g0, b0, alpha)
            new_conv.append(st)
        else:
            h2 = h.reshape(t, d)
            if layer == n_a:
                k2, v2, kb, vb = _kv_proj(h2, wts["kv_w"])
                k_new = k2.reshape(bsz, seq, d // SB_HEAD_DIM, SB_HEAD_DIM)
                v_new = v2.reshape(bsz, seq, d // SB_HEAD_DIM, SB_HEAD_DIM)
                k_all, v_all = kb.reshape(bsz, seq, d), vb.reshape(bsz, seq, d)
                if k_past is not None:
                    past = k_past.shape[1]
                    k_all = jnp.concatenate([k_past.reshape(bsz, past, d).astype(BF16), k_all], axis=1)
                    v_all = jnp.concatenate([v_past.reshape(bsz, past, d).astype(BF16), v_all], axis=1)
            j = layer - n_a
            q = _q_proj(h2, wts["sb_w_q"][j]).reshape(bsz, seq, d)
            att = _sb_attention(q, k_all, v_all, q_pos0)
            h = _oproj_ln(att.reshape(t, d), h2, wts["sb_w_o"][j], g0, b0, alpha).reshape(bsz, seq, d)
        h2 = h.reshape(t, d)
        ri, rj, rw = _peer_route(h2, wts["peer_wq_t"][layer], wts["peer_sk"][layer])
        h = _peer_experts(h2, ri, rj, rw, wts["peer_u"][layer], wts["peer_v"][layer],
                          g1, b1, alpha).reshape(bsz, seq, d)
    return h, jnp.stack(new_conv, axis=0), k_new, v_new


def kernel(x_prompt, x_sample, state_conv, cache_k, cache_v, conv_w_in, conv_w_dw, conv_w_out,
           sb_w_q, sb_w_o, kv_w_k, kv_w_v, peer_w_q, peer_subkeys, peer_u, peer_v, ln_g, ln_b):
    depth = peer_w_q.shape[0]
    wts = {
        "conv_w_in": conv_w_in.astype(BF16),
        "conv_w_dw": conv_w_dw,
        "conv_w_out": conv_w_out.astype(BF16),
        "sb_w_q": sb_w_q.astype(BF16),
        "sb_w_o": sb_w_o.astype(BF16),
        "kv_w": jnp.concatenate([kv_w_k, kv_w_v], axis=1).astype(BF16),
        "peer_wq_t": jnp.swapaxes(peer_w_q, 1, 2).astype(BF16),
        "peer_sk": peer_subkeys.reshape(depth, 2 * PEER_HEADS, PEER_N_KEYS, PEER_HALF).astype(BF16),
        "peer_u": peer_u.astype(BF16),
        "peer_v": peer_v.astype(BF16),
        "ln_g": ln_g,
        "ln_b": ln_b,
    }
    n_a = conv_w_in.shape[0]
    zero_conv = jnp.zeros((n_a, x_prompt.shape[0], state_conv.shape[2], x_prompt.shape[2]), x_prompt.dtype)
    y_p, conv_p, k_p, v_p = _trunk(x_prompt, zero_conv, None, None, 0, wts)
    y_s, conv_s, k_s, v_s = _trunk(x_sample, state_conv, cache_k, cache_v, cache_k.shape[1], wts)
    return (y_p, y_s, conv_p, k_p, v_p, conv_s, k_s, v_s)
```

```python
import functools
import math

import jax
import jax.numpy as jnp
from jax import lax
from jax.experimental import pallas as pl
from jax.experimental.pallas import tpu as pltpu

F32 = jnp.float32
BF16 = jnp.bfloat16
I32 = jnp.int32

LN_EPS = 1e-5
SB_HEAD_DIM = 64
PEER_HEADS = 8
PEER_N_KEYS = 128
PEER_TOPK = 16
PEER_HALF = 128
LANES = 128
G_PITCH = 136
VMEM_LIMIT = 56 * 1024 * 1024
EXP_UNDERFLOW = -104.0
G_PAIRS = 64
SB_HEADS_PER_STEP = 2

ROW_TILE = 512
SB_Q_TILE = 256
SB_K_BLOCK = 512
SB_KEY_ALIGN = 16
ROUTE_TILE = 2048
EXPERT_TILE = 512
EXPERT_CHUNK = 2048
EXPERT_SUB = 1024


def _cparams(sem):
    return pltpu.CompilerParams(dimension_semantics=sem, vmem_limit_bytes=VMEM_LIMIT)


def _pack_bf16_pair(lo, hi):
    lo_bits = lax.bitcast_convert_type(lo.astype(BF16).astype(F32), jnp.uint32) >> 16
    hi_bits = lax.bitcast_convert_type(hi.astype(BF16).astype(F32), jnp.uint32) & jnp.uint32(0xFFFF0000)
    return lo_bits | hi_bits


def _unpack_bf16_pair(packed):
    lo = lax.bitcast_convert_type(packed << 16, F32)
    hi = lax.bitcast_convert_type(packed & jnp.uint32(0xFFFF0000), F32)
    return lo, hi


def _layernorm(v, g, b):
    mu = jnp.mean(v, axis=-1, keepdims=True)
    d = v - mu
    var = jnp.mean(d * d, axis=-1, keepdims=True)
    return d * lax.rsqrt(var + LN_EPS) * g + b


def _conv_kernel(x_ref, prev_ref, win_ref, wdw_ref, wout_ref, g_ref, b_ref,
                 o_ref, st_ref, carry_ref, *, alpha, n_seq_tiles):
    s = pl.program_id(1)
    x = x_ref[0]
    tm, d = x.shape
    proj = jnp.dot(x.astype(BF16), win_ref[...], preferred_element_type=F32)
    b_gate = proj[:, :d]
    u = proj[:, d:2 * d] * proj[:, 2 * d:]

    @pl.when(s == 0)
    def _():
        carry_ref[0:2, :] = prev_ref[0]

    p0 = carry_ref[0:1, :]
    p1 = carry_ref[1:2, :]
    row = lax.broadcasted_iota(I32, (tm, d), 0)
    um1 = jnp.where(row == 0, p1, pltpu.roll(u, 1, 0))
    um2 = jnp.where(row == 0, p0, jnp.where(row == 1, p1, pltpu.roll(u, 2, 0)))
    acc = wdw_ref[0:1, :] * um2 + wdw_ref[1:2, :] * um1 + wdw_ref[2:3, :] * u
    carry_ref[0:2, :] = u[tm - 2:tm, :]
    y = jnp.dot((b_gate * acc).astype(BF16), wout_ref[...], preferred_element_type=F32)
    o_ref[0] = _layernorm(alpha * x + y, g_ref[...], b_ref[...])

    @pl.when(s == n_seq_tiles - 1)
    def _():
        st_ref[0] = u[tm - 2:tm, :]


def _conv_layer(h, prev, w_in, w_dw, w_out, g, b, alpha):
    bsz, seq, d = h.shape
    tm = min(seq, ROW_TILE)
    ns = seq // tm
    return pl.pallas_call(
        functools.partial(_conv_kernel, alpha=alpha, n_seq_tiles=ns),
        grid=(bsz, ns),
        in_specs=[
            pl.BlockSpec((1, tm, d), lambda i, j: (i, j, 0)),
            pl.BlockSpec((1, 2, d), lambda i, j: (i, 0, 0)),
            pl.BlockSpec((d, 3 * d), lambda i, j: (0, 0)),
            pl.BlockSpec((3, d), lambda i, j: (0, 0)),
            pl.BlockSpec((d, d), lambda i, j: (0, 0)),
            pl.BlockSpec((1, d), lambda i, j: (0, 0)),
            pl.BlockSpec((1, d), lambda i, j: (0, 0)),
        ],
        out_specs=[
            pl.BlockSpec((1, tm, d), lambda i, j: (i, j, 0)),
            pl.BlockSpec((1, 2, d), lambda i, j: (i, 0, 0)),
        ],
        out_shape=[
            jax.ShapeDtypeStruct((bsz, seq, d), F32),
            jax.ShapeDtypeStruct((bsz, 2, d), F32),
        ],
        scratch_shapes=[pltpu.VMEM((8, d), F32)],
        name="conv_mixer",
        compiler_params=_cparams(("arbitrary", "arbitrary")),
    )(h, prev, w_in, w_dw, w_out, g, b)


def _kv_kernel(x_ref, w_ref, k_ref, v_ref, kb_ref, vb_ref):
    d = x_ref.shape[1]
    y = jnp.dot(x_ref[...].astype(BF16), w_ref[...], preferred_element_type=F32)
    k = y[:, :d]
    v = y[:, d:]
    k_ref[...] = k
    v_ref[...] = v
    kb_ref[...] = k.astype(BF16)
    vb_ref[...] = v.astype(BF16)


def _kv_proj(h2, w_kv):
    t, d = h2.shape
    tm = min(t, ROW_TILE)
    blk = pl.BlockSpec((tm, d), lambda i: (i, 0))
    return pl.pallas_call(
        _kv_kernel,
        grid=(t // tm,),
        in_specs=[blk, pl.BlockSpec((d, 2 * d), lambda i: (0, 0))],
        out_specs=[blk, blk, blk, blk],
        out_shape=[
            jax.ShapeDtypeStruct((t, d), F32), jax.ShapeDtypeStruct((t, d), F32),
            jax.ShapeDtypeStruct((t, d), BF16), jax.ShapeDtypeStruct((t, d), BF16),
        ],
        name="kv_proj",
        compiler_params=_cparams(("arbitrary",)),
    )(h2, w_kv)


def _q_kernel(x_ref, w_ref, o_ref, *, scale):
    o_ref[...] = (jnp.dot(x_ref[...].astype(BF16), w_ref[...],
                          preferred_element_type=F32) * scale).astype(o_ref.dtype)


def _q_proj(h2, w_q):
    t, d = h2.shape
    tm = min(t, ROW_TILE)
    blk = pl.BlockSpec((tm, d), lambda i: (i, 0))
    scale = SB_HEAD_DIM ** -0.5
    assert math.frexp(scale)[0] == 0.5, scale
    return pl.pallas_call(
        functools.partial(_q_kernel, scale=scale),
        grid=(t // tm,),
        in_specs=[blk, pl.BlockSpec((d, d), lambda i: (0, 0))],
        out_specs=blk,
        out_shape=jax.ShapeDtypeStruct((t, d), BF16),
        name="q_proj",
        compiler_params=_cparams(("arbitrary",)),
    )(h2, w_q)


def _oproj_ln_kernel(a_ref, h_ref, w_ref, g_ref, b_ref, o_ref, *, alpha):
    y = jnp.dot(a_ref[...], w_ref[...], preferred_element_type=F32)
    o_ref[...] = _layernorm(alpha * h_ref[...] + y, g_ref[...], b_ref[...])


def _oproj_ln(a2, h2, w_o, g, b, alpha):
    t, d = h2.shape
    tm = min(t, ROW_TILE)
    blk = pl.BlockSpec((tm, d), lambda i: (i, 0))
    vec = pl.BlockSpec((1, d), lambda i: (0, 0))
    return pl.pallas_call(
        functools.partial(_oproj_ln_kernel, alpha=alpha),
        grid=(t // tm,),
        in_specs=[blk, blk, pl.BlockSpec((d, d), lambda i: (0, 0)), vec, vec],
        out_specs=blk,
        out_shape=jax.ShapeDtypeStruct((t, d), F32),
        name="oproj_ln",
        compiler_params=_cparams(("arbitrary",)),
    )(a2, h2, w_o, g, b)


def _sb_kernel(q_ref, k_ref, v_ref, tri_ref, o_ref, *, q_pos0, tq, tk, width):
    for group in range(q_ref.shape[2] // width):
        _sb_head_group(q_ref, k_ref, v_ref, tri_ref, o_ref, slice(group * width, (group + 1) * width),
                       q_pos0=q_pos0, tq=tq, tk=tk)


def _sb_head_group(q_ref, k_ref, v_ref, tri_ref, o_ref, lanes, *, q_pos0, tq, tk):
    qi = pl.program_id(1)
    q_all = q_ref[0, :, lanes]
    width = q_all.shape[1]
    n_heads = width // SB_HEAD_DIM
    head_of_lane = lax.broadcasted_iota(I32, (1, width), 1) // SB_HEAD_DIM
    zero = jnp.zeros_like(q_all)
    q_heads = [jnp.where(head_of_lane == hd, q_all, zero) for hd in range(n_heads)]
    q_start = q_pos0 + qi * tq
    q_end = q_start + tq
    qpos = q_start + lax.broadcasted_iota(I32, (tq, 1), 0)
    nblk = (q_end + tk - 1) // tk

    def cond(carry):
        return jnp.logical_and(carry[0] < nblk, carry[1] > 0)

    def body(carry):
        j = carry[0]
        hi = q_end - j * tk
        start = pl.multiple_of(jnp.maximum(hi - tk, 0), SB_KEY_ALIGN)
        kblk = k_ref[0, pl.ds(start, tk), lanes]
        vblk = v_ref[0, pl.ds(start, tk), lanes]
        kpos = start + lax.broadcasted_iota(I32, (1, tk), 1)
        causal = kpos < jnp.minimum(qpos, hi)
        out = []
        for hd in range(n_heads):
            acc, run = carry[2 + 2 * hd], carry[3 + 2 * hd]
            z = lax.dot_general(q_heads[hd], kblk, (((1,), (1,)), ((), ())),
                                preferred_element_type=F32)
            log_sig = jnp.minimum(z, 0.0) - jnp.log(1.0 + jnp.exp(-jnp.abs(z)))
            log_not = jnp.where(causal, log_sig - z, 0.0)
            suffix = jnp.dot(log_not.astype(BF16), tri_ref[...],
                             preferred_element_type=F32) + run
            a = jnp.where(causal, jnp.exp(log_sig + suffix), 0.0)
            acc = acc + jnp.dot(a.astype(BF16), vblk, preferred_element_type=F32)
            run = run + jnp.sum(log_not, axis=1, keepdims=True)
            out += [acc, run]
        live = (jnp.max(functools.reduce(jnp.maximum, out[1::2])) > EXP_UNDERFLOW).astype(I32)
        return (j + 1, live, *out)

    init = (jnp.int32(0), jnp.int32(1)) + (jnp.zeros((tq, width), F32), jnp.zeros((tq, 1), F32)) * n_heads
    res = lax.while_loop(cond, body, init)
    result = res[2]
    for hd in range(1, n_heads):
        result = jnp.where(head_of_lane == hd, res[2 + 2 * hd], result)
    o_ref[0, :, lanes] = result.astype(o_ref.dtype)


def _sb_attention(q, k_all, v_all, q_pos0):
    bsz, sq, d = q.shape
    tq = min(sq, SB_Q_TILE)
    tk = SB_K_BLOCK
    sk = k_all.shape[1]
    tri = (lax.broadcasted_iota(I32, (tk, tk), 0) >
           lax.broadcasted_iota(I32, (tk, tk), 1)).astype(BF16)
    width = SB_HEADS_PER_STEP * SB_HEAD_DIM
    assert d % width == 0 and width % LANES == 0, (d, width)
    assert sk == q_pos0 + sq and sk >= tk and q_pos0 % SB_KEY_ALIGN == 0 and tq % SB_KEY_ALIGN == 0, (sk, q_pos0, sq)
    return pl.pallas_call(
        functools.partial(_sb_kernel, q_pos0=q_pos0, tq=tq, tk=tk, width=width),
        grid=(bsz, sq // tq),
        in_specs=[
            pl.BlockSpec((1, tq, d), lambda b, i: (b, i, 0)),
            pl.BlockSpec((1, sk, d), lambda b, i: (b, 0, 0)),
            pl.BlockSpec((1, sk, d), lambda b, i: (b, 0, 0)),
            pl.BlockSpec((tk, tk), lambda b, i: (0, 0)),
        ],
        out_specs=pl.BlockSpec((1, tq, d), lambda b, i: (b, i, 0)),
        out_shape=jax.ShapeDtypeStruct((bsz, sq, d), BF16),
        name="sb_attention",
        compiler_params=_cparams(("arbitrary", "arbitrary")),
    )(q, k_all, v_all, tri)


def _sort_network(n):
    pairs = []
    p = 1
    while p < n:
        k = p
        while k >= 1:
            for j in range(k % p, n - k, 2 * k):
                for i in range(min(k, n - j - k)):
                    if (i + j) // (2 * p) == (i + j + k) // (2 * p):
                        pairs.append((i + j, i + j + k))
            k //= 2
        p *= 2
    return pairs


def _pop_merge(lists, ranks, steps, big, singles=None):
    lists, ranks = list(lists), list(ranks)
    vals, ids = [], []
    for step in range(steps):
        head, head_rank = lists[0], ranks[0]
        if singles is not None:
            head = jnp.concatenate([head, singles[0]], axis=0)
            head_rank = jnp.concatenate([head_rank, singles[1]], axis=0)
        m = jnp.max(head, axis=0, keepdims=True)
        sel = jnp.min(jnp.where(head == m, head_rank, big), axis=0, keepdims=True)
        if singles is not None:
            singles = (jnp.where(singles[1] == sel, -jnp.inf, singles[0]), singles[1])
        vals.append(m)
        ids.append(sel)
        won = ranks[0] == sel
        for r in range(min(steps - 1 - step, len(lists) - 1)):
            lists[r] = jnp.where(won, lists[r + 1], lists[r])
            ranks[r] = jnp.where(won, ranks[r + 1], ranks[r])
    return jnp.concatenate(vals, axis=0), jnp.concatenate(ids, axis=0)


def _top16_rows(s):
    n_slabs = s.shape[0] // 8
    sub = lax.broadcasted_iota(I32, (8, s.shape[1]), 0)
    vals = [s[8 * r:8 * r + 8] for r in range(n_slabs)]
    keys = [sub + 8 * r for r in range(n_slabs)]
    for i, j in _sort_network(n_slabs):
        a, b, ka, kb = vals[i], vals[j], keys[i], keys[j]
        swap = (b > a) | ((b == a) & (kb < ka))
        vals[i], vals[j] = jnp.where(swap, b, a), jnp.where(swap, a, b)
        keys[i], keys[j] = jnp.where(swap, kb, ka), jnp.where(swap, ka, kb)
    return _pop_merge(vals, keys, PEER_TOPK, PEER_N_KEYS)


def _product_top16(v1, n1, v2, n2):
    k = PEER_TOPK
    a_low = lax.broadcasted_iota(I32, (8, v1.shape[1]), 0)
    lists, ranks = [], []
    for b in range(k):
        longest = k // (b + 1)
        lists.append(jnp.where(a_low < longest, v1[0:8] + v2[b:b + 1], -jnp.inf))
        ranks.append(a_low * k + b)
    singles = (v1[8:k] + v2[0:1], (a_low + 8) * k)
    best, code = _pop_merge(lists, ranks, k, k * k, singles)
    rank_a, rank_b = code >> 4, code & (PEER_TOPK - 1)
    bi = jnp.zeros_like(code)
    bj = jnp.zeros_like(code)
    for r in range(PEER_TOPK):
        bi = jnp.where(rank_a == r, n1[r:r + 1], bi)
        bj = jnp.where(rank_b == r, n2[r:r + 1], bj)
    return best, bi, bj


def _route_kernel(h_ref, wq_ref, sk_ref, i_ref, j_ref, w_ref,
                  q_s, v_s, n_s, it_s, jt_s, wt_s):
    tr = h_ref.shape[0]
    nsub = tr // LANES
    x = h_ref[...].astype(BF16)
    q_t = lax.dot_general(wq_ref[...], x, (((1,), (1,)), ((), ())),
                          preferred_element_type=F32)
    q_s[...] = q_t.astype(BF16)

    def half_body(hp, _):
        qs = q_s[pl.ds(pl.multiple_of(hp * PEER_HALF, PEER_HALF), PEER_HALF), :]
        s_t = jnp.dot(sk_ref[hp], qs, preferred_element_type=F32)
        for sub in range(nsub):
            sl = slice(sub * LANES, (sub + 1) * LANES)
            v, n = _top16_rows(s_t[:, sl])
            v_s[hp, :, sl] = v
            n_s[hp, :, sl] = n
        return 0

    lax.fori_loop(0, 2 * PEER_HEADS, half_body, 0)

    def head_body(hd, _):
        for sub in range(nsub):
            sl = slice(sub * LANES, (sub + 1) * LANES)
            v1, v2 = v_s[2 * hd, :, sl], v_s[2 * hd + 1, :, sl]
            n1, n2 = n_s[2 * hd, :, sl], n_s[2 * hd + 1, :, sl]
            best, bi, bj = _product_top16(v1, n1, v2, n2)
            e = jnp.exp(best - best[0:1])
            gate = e / jnp.sum(e, axis=0, keepdims=True)
            rows = pl.ds(pl.multiple_of(hd * PEER_TOPK, PEER_TOPK), PEER_TOPK)
            it_s[rows, sl] = bi
            jt_s[rows, sl] = bj
            wt_s[rows, sl] = gate
        return 0

    lax.fori_loop(0, PEER_HEADS, head_body, 0)
    i_ref[...] = it_s[...].T
    j_ref[...] = jt_s[...].T
    w_ref[...] = wt_s[...].T


def _peer_route(h2, wq_t, sk):
    t, d = h2.shape
    tr = min(t, ROUTE_TILE)
    assert t % tr == 0 and tr % LANES == 0, (t, tr)
    nq = wq_t.shape[0]
    hk = PEER_HEADS * PEER_TOPK
    blk = pl.BlockSpec((tr, hk), lambda i: (i, 0))
    return pl.pallas_call(
        _route_kernel,
        grid=(t // tr,),
        in_specs=[
            pl.BlockSpec((tr, d), lambda i: (i, 0)),
            pl.BlockSpec((nq, d), lambda i: (0, 0)),
            pl.BlockSpec(sk.shape, lambda i: (0, 0, 0)),
        ],
        out_specs=[blk, blk, blk],
        out_shape=[jax.ShapeDtypeStruct((t, hk), I32), jax.ShapeDtypeStruct((t, hk), I32),
                   jax.ShapeDtypeStruct((t, hk), F32)],
        scratch_shapes=[
            pltpu.VMEM((nq, tr), BF16),
            pltpu.VMEM((2 * PEER_HEADS, PEER_TOPK, tr), F32),
            pltpu.VMEM((2 * PEER_HEADS, PEER_TOPK, tr), I32),
            pltpu.VMEM((hk, tr), I32),
            pltpu.VMEM((hk, tr), I32),
            pltpu.VMEM((hk, tr), F32),
        ],
        name="peer_route",
        compiler_params=_cparams(("arbitrary",)),
    )(h2, wq_t, sk)


def _expert_kernel(h_ref, i_ref, j_ref, w_ref, u_ref, v_ref, g_ref, b_ref, o_ref,
                   xb_s, gate_s, acc_s, *, alpha, n_chunks, sub_e):
    c = pl.program_id(1)
    tm = h_ref.shape[0]
    half = tm // 2
    ec = u_ref.shape[0]
    hk = i_ref.shape[1]

    @pl.when(c == 0)
    def _():
        xb_s[...] = h_ref[...].astype(BF16)
        acc_s[...] = jnp.zeros_like(acc_s)
        key = lax.broadcasted_iota(I32, (PEER_N_KEYS, hk), 0)
        zeros = jnp.zeros((PEER_N_KEYS, hk), BF16)

        def group_body(p, _):
            k0 = pl.multiple_of(p * G_PAIRS, G_PAIRS)
            k1 = pl.multiple_of(half + k0, G_PAIRS)
            i_blk = (i_ref[pl.ds(k0, G_PAIRS), :], i_ref[pl.ds(k1, G_PAIRS), :])
            j_blk = (j_ref[pl.ds(k0, G_PAIRS), :], j_ref[pl.ds(k1, G_PAIRS), :])
            w_blk = (w_ref[pl.ds(k0, G_PAIRS), :], w_ref[pl.ds(k1, G_PAIRS), :])
            for pair in range(G_PAIRS):
                lhs, rhs = [], []
                for side in range(2):
                    it = jnp.broadcast_to(i_blk[side][pair:pair + 1, :], (PEER_N_KEYS, hk))
                    jt = jnp.broadcast_to(j_blk[side][pair:pair + 1, :], (PEER_N_KEYS, hk))
                    wt = jnp.broadcast_to(w_blk[side][pair:pair + 1, :], (PEER_N_KEYS, hk))
                    lhs.append(jnp.where(key == it, 1.0, 0.0).astype(BF16))
                    rhs.append(jnp.where(key == jt, wt, 0.0).astype(BF16))
                x_mat = jnp.concatenate(lhs, axis=1)
                y_mat = jnp.concatenate(
                    [jnp.concatenate([rhs[0], zeros], axis=1),
                     jnp.concatenate([zeros, rhs[1]], axis=1)], axis=0)
                gp = lax.dot_general(x_mat, y_mat, (((1,), (1,)), ((), ())),
                                     preferred_element_type=F32)
                base = pl.multiple_of((k0 + pair) * G_PITCH, 8)
                gate_s[pl.ds(base, PEER_N_KEYS), :] = _pack_bf16_pair(gp[:, :LANES], gp[:, LANES:])
            return 0

        lax.fori_loop(0, half // G_PAIRS, group_body, 0)

    i_base = c * (ec // PEER_N_KEYS)
    per_sub = sub_e // PEER_N_KEYS
    for sc in range(ec // sub_e):
        a = lax.dot_general(xb_s[...], u_ref[sc * sub_e:(sc + 1) * sub_e, :], (((1,), (1,)), ((), ())),
                            preferred_element_type=F32)
        gate = jnp.concatenate(
            [jnp.concatenate(_unpack_bf16_pair(
                gate_s[pl.ds(i_base + sc * per_sub + r, half, stride=G_PITCH), :]), axis=0)
             for r in range(per_sub)], axis=1)
        act = 0.5 * a * (1.0 + lax.erf(a * (2.0 ** -0.5)))
        hm = (act * gate).astype(BF16)
        acc_s[...] += jnp.dot(hm, v_ref[sc * sub_e:(sc + 1) * sub_e, :],
                              preferred_element_type=F32)

    @pl.when(c == n_chunks - 1)
    def _():
        o_ref[...] = _layernorm(alpha * h_ref[...] + acc_s[...], g_ref[...], b_ref[...])


def _peer_experts(h2, ri, rj, rw, u_tab, v_tab, g, b, alpha):
    t, d = h2.shape
    n_exp = v_tab.shape[0]
    tm = min(t, EXPERT_TILE)
    ec = EXPERT_CHUNK
    assert t % tm == 0 and tm % (2 * G_PAIRS) == 0 and n_exp % ec == 0, (t, n_exp)
    nc = n_exp // ec
    hk = ri.shape[1]
    tok = pl.BlockSpec((tm, d), lambda i, c: (i, 0))
    rt = pl.BlockSpec((tm, hk), lambda i, c: (i, 0))
    vec = pl.BlockSpec((1, d), lambda i, c: (0, 0))
    return pl.pallas_call(
        functools.partial(_expert_kernel, alpha=alpha, n_chunks=nc, sub_e=EXPERT_SUB),
        grid=(t // tm, nc),
        in_specs=[tok, rt, rt, rt,
                  pl.BlockSpec((ec, d), lambda i, c: (c, 0)),
                  pl.BlockSpec((ec, d), lambda i, c: (c, 0)),
                  vec, vec],
        out_specs=tok,
        out_shape=jax.ShapeDtypeStruct((t, d), F32),
        scratch_shapes=[
            pltpu.VMEM((tm, d), BF16),
            pltpu.VMEM((tm // 2 * G_PITCH, LANES), jnp.uint32),
            pltpu.VMEM((tm, d), F32),
        ],
        name="peer_experts",
        compiler_params=_cparams(("arbitrary", "arbitrary")),
    )(h2, ri, rj, rw, u_tab, v_tab, g, b)


def _trunk(x, conv_prev, k_past, v_past, q_pos0, wts):
    bsz, seq, d = x.shape
    t = bsz * seq
    n_a = wts["conv_w_in"].shape[0]
    depth = n_a + wts["sb_w_q"].shape[0]
    alpha = (2.0 * depth) ** 0.25
    h = x
    new_conv = []
    k_new = v_new = k_all = v_all = None
    for layer in range(depth):
        g0, b0 = wts["ln_g"][layer, 0][None], wts["ln_b"][layer, 0][None]
        g1, b1 = wts["ln_g"][layer, 1][None], wts["ln_b"][layer, 1][None]
        if layer < n_a:
            h, st = _conv_layer(h, conv_prev[layer], wts["conv_w_in"][layer],
                                wts["conv_w_dw"][layer], wts["conv_w_out"][layer], g0, b0, alpha)
            new_conv.append(st)
        else:
            h2 = h.reshape(t, d)
            if layer == n_a:
                k2, v2, kb, vb = _kv_proj(h2, wts["kv_w"])
                k_new = k2.reshape(bsz, seq, d // SB_HEAD_DIM, SB_HEAD_DIM)
                v_new = v2.reshape(bsz, seq, d // SB_HEAD_DIM, SB_HEAD_DIM)
                k_all, v_all = kb.reshape(bsz, seq, d), vb.reshape(bsz, seq, d)
                if k_past is not None:
                    past = k_past.shape[1]
                    k_all = jnp.concatenate([k_past.reshape(bsz, past, d).astype(BF16), k_all], axis=1)
                    v_all = jnp.concatenate([v_past.reshape(bsz, past, d).astype(BF16), v_all], axis=1)
            j = layer - n_a
            q = _q_proj(h2, wts["sb_w_q"][j]).reshape(bsz, seq, d)
            att = _sb_attention(q, k_all, v_all, q_pos0)
            h = _oproj_ln(att.reshape(t, d), h2, wts["sb_w_o"][j], g0, b0, alpha).reshape(bsz, seq, d)
        h2 = h.reshape(t, d)
        ri, rj, rw = _peer_route(h2, wts["peer_wq_t"][layer], wts["peer_sk"][layer])
        h = _peer_experts(h2, ri, rj, rw, wts["peer_u"][layer], wts["peer_v"][layer],
                          g1, b1, alpha).reshape(bsz, seq, d)
    return h, jnp.stack(new_conv, axis=0), k_new, v_new


def kernel(x_prompt, x_sample, state_conv, cache_k, cache_v, conv_w_in, conv_w_dw, conv_w_out,
           sb_w_q, sb_w_o, kv_w_k, kv_w_v, peer_w_q, peer_subkeys, peer_u, peer_v, ln_g, ln_b):
    depth = peer_w_q.shape[0]
    wts = {
        "conv_w_in": conv_w_in.astype(BF16),
        "conv_w_dw": conv_w_dw,
        "conv_w_out": conv_w_out.astype(BF16),
        "sb_w_q": sb_w_q.astype(BF16),
        "sb_w_o": sb_w_o.astype(BF16),
        "kv_w": jnp.concatenate([kv_w_k, kv_w_v], axis=1).astype(BF16),
        "peer_wq_t": jnp.swapaxes(peer_w_q, 1, 2).astype(BF16),
        "peer_sk": peer_subkeys.reshape(depth, 2 * PEER_HEADS, PEER_N_KEYS, PEER_HALF).astype(BF16),
        "peer_u": peer_u.astype(BF16),
        "peer_v": peer_v.astype(BF16),
        "ln_g": ln_g,
        "ln_b": ln_b,
    }
    n_a = conv_w_in.shape[0]
    zero_conv = jnp.zeros((n_a, x_prompt.shape[0], state_conv.shape[2], x_prompt.shape[2]), x_prompt.dtype)
    y_p, conv_p, k_p, v_p = _trunk(x_prompt, zero_conv, None, None, 0, wts)
    y_s, conv_s, k_s, v_s = _trunk(x_sample, state_conv, cache_k, cache_v, cache_k.shape[1], wts)
    return (y_p, y_s, conv_p, k_p, v_p, conv_s, k_s, v_s)
```

```python
import functools
import math

import jax
import jax.numpy as jnp
from jax import lax
from jax.experimental import pallas as pl
from jax.experimental.pallas import tpu as pltpu

F32 = jnp.float32
BF16 = jnp.bfloat16
I32 = jnp.int32

LN_EPS = 1e-5
SB_HEAD_DIM = 64
PEER_HEADS = 8
PEER_N_KEYS = 128
PEER_TOPK = 16
PEER_HALF = 128
LANES = 128
G_PITCH = 136
VMEM_LIMIT = 56 * 1024 * 1024
EXP_UNDERFLOW = -104.0
G_PAIRS = 64
SB_HEADS_PER_STEP = 4

ROW_TILE = 512
SB_Q_TILE = 256
SB_K_BLOCK = 512
SB_KEY_ALIGN = 16
ROUTE_TILE = 2048
EXPERT_TILE = 512
EXPERT_CHUNK = 2048
EXPERT_SUB = 1024


def _cparams(sem):
    return pltpu.CompilerParams(dimension_semantics=sem, vmem_limit_bytes=VMEM_LIMIT)


def _pack_bf16_pair(lo, hi):
    lo_bits = lax.bitcast_convert_type(lo.astype(BF16).astype(F32), jnp.uint32) >> 16
    hi_bits = lax.bitcast_convert_type(hi.astype(BF16).astype(F32), jnp.uint32) & jnp.uint32(0xFFFF0000)
    return lo_bits | hi_bits


def _unpack_bf16_pair(packed):
    lo = lax.bitcast_convert_type(packed << 16, F32)
    hi = lax.bitcast_convert_type(packed & jnp.uint32(0xFFFF0000), F32)
    return lo, hi


def _layernorm(v, g, b):
    mu = jnp.mean(v, axis=-1, keepdims=True)
    d = v - mu
    var = jnp.mean(d * d, axis=-1, keepdims=True)
    return d * lax.rsqrt(var + LN_EPS) * g + b


def _conv_kernel(x_ref, prev_ref, win_ref, wdw_ref, wout_ref, g_ref, b_ref,
                 o_ref, st_ref, carry_ref, *, alpha, n_seq_tiles):
    s = pl.program_id(1)
    x = x_ref[0]
    tm, d = x.shape
    proj = jnp.dot(x.astype(BF16), win_ref[...], preferred_element_type=F32)
    b_gate = proj[:, :d]
    u = proj[:, d:2 * d] * proj[:, 2 * d:]

    @pl.when(s == 0)
    def _():
        carry_ref[0:2, :] = prev_ref[0]

    p0 = carry_ref[0:1, :]
    p1 = carry_ref[1:2, :]
    row = lax.broadcasted_iota(I32, (tm, d), 0)
    um1 = jnp.where(row == 0, p1, pltpu.roll(u, 1, 0))
    um2 = jnp.where(row == 0, p0, jnp.where(row == 1, p1, pltpu.roll(u, 2, 0)))
    acc = wdw_ref[0:1, :] * um2 + wdw_ref[1:2, :] * um1 + wdw_ref[2:3, :] * u
    carry_ref[0:2, :] = u[tm - 2:tm, :]
    y = jnp.dot((b_gate * acc).astype(BF16), wout_ref[...], preferred_element_type=F32)
    o_ref[0] = _layernorm(alpha * x + y, g_ref[...], b_ref[...])

    @pl.when(s == n_seq_tiles - 1)
    def _():
        st_ref[0] = u[tm - 2:tm, :]


def _conv_layer(h, prev, w_in, w_dw, w_out, g, b, alpha):
    bsz, seq, d = h.shape
    tm = min(seq, ROW_TILE)
    ns = seq // tm
    return pl.pallas_call(
        functools.partial(_conv_kernel, alpha=alpha, n_seq_tiles=ns),
        grid=(bsz, ns),
        in_specs=[
            pl.BlockSpec((1, tm, d), lambda i, j: (i, j, 0)),
            pl.BlockSpec((1, 2, d), lambda i, j: (i, 0, 0)),
            pl.BlockSpec((d, 3 * d), lambda i, j: (0, 0)),
            pl.BlockSpec((3, d), lambda i, j: (0, 0)),
            pl.BlockSpec((d, d), lambda i, j: (0, 0)),
            pl.BlockSpec((1, d), lambda i, j: (0, 0)),
            pl.BlockSpec((1, d), lambda i, j: (0, 0)),
        ],
        out_specs=[
            pl.BlockSpec((1, tm, d), lambda i, j: (i, j, 0)),
            pl.BlockSpec((1, 2, d), lambda i, j: (i, 0, 0)),
        ],
        out_shape=[
            jax.ShapeDtypeStruct((bsz, seq, d), F32),
            jax.ShapeDtypeStruct((bsz, 2, d), F32),
        ],
        scratch_shapes=[pltpu.VMEM((8, d), F32)],
        name="conv_mixer",
        compiler_params=_cparams(("arbitrary", "arbitrary")),
    )(h, prev, w_in, w_dw, w_out, g, b)


def _kv_kernel(x_ref, w_ref, k_ref, v_ref, kb_ref, vb_ref):
    d = x_ref.shape[1]
    y = jnp.dot(x_ref[...].astype(BF16), w_ref[...], preferred_element_type=F32)
    k = y[:, :d]
    v = y[:, d:]
    k_ref[...] = k
    v_ref[...] = v
    kb_ref[...] = k.astype(BF16)
    vb_ref[...] = v.astype(BF16)


def _kv_proj(h2, w_kv):
    t, d = h2.shape
    tm = min(t, ROW_TILE)
    blk = pl.BlockSpec((tm, d), lambda i: (i, 0))
    return pl.pallas_call(
        _kv_kernel,
        grid=(t // tm,),
        in_specs=[blk, pl.BlockSpec((d, 2 * d), lambda i: (0, 0))],
        out_specs=[blk, blk, blk, blk],
        out_shape=[
            jax.ShapeDtypeStruct((t, d), F32), jax.ShapeDtypeStruct((t, d), F32),
            jax.ShapeDtypeStruct((t, d), BF16), jax.ShapeDtypeStruct((t, d), BF16),
        ],
        name="kv_proj",
        compiler_params=_cparams(("arbitrary",)),
    )(h2, w_kv)


def _q_kernel(x_ref, w_ref, o_ref, *, scale):
    o_ref[...] = (jnp.dot(x_ref[...].astype(BF16), w_ref[...],
                          preferred_element_type=F32) * scale).astype(o_ref.dtype)


def _q_proj(h2, w_q):
    t, d = h2.shape
    tm = min(t, ROW_TILE)
    blk = pl.BlockSpec((tm, d), lambda i: (i, 0))
    scale = SB_HEAD_DIM ** -0.5
    assert math.frexp(scale)[0] == 0.5, scale
    return pl.pallas_call(
        functools.partial(_q_kernel, scale=scale),
        grid=(t // tm,),
        in_specs=[blk, pl.BlockSpec((d, d), lambda i: (0, 0))],
        out_specs=blk,
        out_shape=jax.ShapeDtypeStruct((t, d), BF16),
        name="q_proj",
        compiler_params=_cparams(("arbitrary",)),
    )(h2, w_q)


def _oproj_ln_kernel(a_ref, h_ref, w_ref, g_ref, b_ref, o_ref, *, alpha):
    y = jnp.dot(a_ref[...], w_ref[...], preferred_element_type=F32)
    o_ref[...] = _layernorm(alpha * h_ref[...] + y, g_ref[...], b_ref[...])


def _oproj_ln(a2, h2, w_o, g, b, alpha):
    t, d = h2.shape
    tm = min(t, ROW_TILE)
    blk = pl.BlockSpec((tm, d), lambda i: (i, 0))
    vec = pl.BlockSpec((1, d), lambda i: (0, 0))
    return pl.pallas_call(
        functools.partial(_oproj_ln_kernel, alpha=alpha),
        grid=(t // tm,),
        in_specs=[blk, blk, pl.BlockSpec((d, d), lambda i: (0, 0)), vec, vec],
        out_specs=blk,
        out_shape=jax.ShapeDtypeStruct((t, d), F32),
        name="oproj_ln",
        compiler_params=_cparams(("arbitrary",)),
    )(a2, h2, w_o, g, b)


def _sb_kernel(q_ref, k_ref, v_ref, tri_ref, o_ref, *, q_pos0, tq, tk, width):
    for group in range(q_ref.shape[2] // width):
        _sb_head_group(q_ref, k_ref, v_ref, tri_ref, o_ref, slice(group * width, (group + 1) * width),
                       q_pos0=q_pos0, tq=tq, tk=tk)


def _sb_head_group(q_ref, k_ref, v_ref, tri_ref, o_ref, lanes, *, q_pos0, tq, tk):
    qi = pl.program_id(1)
    q_all = q_ref[0, :, lanes]
    width = q_all.shape[1]
    n_heads = width // SB_HEAD_DIM
    head_of_lane = lax.broadcasted_iota(I32, (1, width), 1) // SB_HEAD_DIM
    zero = jnp.zeros_like(q_all)
    q_heads = [jnp.where(head_of_lane == hd, q_all, zero) for hd in range(n_heads)]
    q_start = q_pos0 + qi * tq
    q_end = q_start + tq
    qpos = q_start + lax.broadcasted_iota(I32, (tq, 1), 0)
    nblk = (q_end + tk - 1) // tk

    def cond(carry):
        return jnp.logical_and(carry[0] < nblk, carry[1] > 0)

    def body(carry):
        j = carry[0]
        hi = q_end - j * tk
        start = pl.multiple_of(jnp.maximum(hi - tk, 0), SB_KEY_ALIGN)
        kblk = k_ref[0, pl.ds(start, tk), lanes]
        vblk = v_ref[0, pl.ds(start, tk), lanes]
        kpos = start + lax.broadcasted_iota(I32, (1, tk), 1)
        causal = kpos < jnp.minimum(qpos, hi)
        out = []
        for hd in range(n_heads):
            acc, run = carry[2 + 2 * hd], carry[3 + 2 * hd]
            z = lax.dot_general(q_heads[hd], kblk, (((1,), (1,)), ((), ())),
                                preferred_element_type=F32)
            log_sig = jnp.minimum(z, 0.0) - jnp.log(1.0 + jnp.exp(-jnp.abs(z)))
            log_not = jnp.where(causal, log_sig - z, 0.0)
            suffix = jnp.dot(log_not.astype(BF16), tri_ref[...],
                             preferred_element_type=F32) + run
            a = jnp.where(causal, jnp.exp(log_sig + suffix), 0.0)
            acc = acc + jnp.dot(a.astype(BF16), vblk, preferred_element_type=F32)
            run = run + jnp.sum(log_not, axis=1, keepdims=True)
            out += [acc, run]
        live = (jnp.max(functools.reduce(jnp.maximum, out[1::2])) > EXP_UNDERFLOW).astype(I32)
        return (j + 1, live, *out)

    init = (jnp.int32(0), jnp.int32(1)) + (jnp.zeros((tq, width), F32), jnp.zeros((tq, 1), F32)) * n_heads
    res = lax.while_loop(cond, body, init)
    result = res[2]
    for hd in range(1, n_heads):
        result = jnp.where(head_of_lane == hd, res[2 + 2 * hd], result)
    o_ref[0, :, lanes] = result.astype(o_ref.dtype)


def _sb_attention(q, k_all, v_all, q_pos0):
    bsz, sq, d = q.shape
    tq = min(sq, SB_Q_TILE)
    tk = SB_K_BLOCK
    sk = k_all.shape[1]
    tri = (lax.broadcasted_iota(I32, (tk, tk), 0) >
           lax.broadcasted_iota(I32, (tk, tk), 1)).astype(BF16)
    width = SB_HEADS_PER_STEP * SB_HEAD_DIM
    assert d % width == 0 and width % LANES == 0, (d, width)
    assert sk == q_pos0 + sq and sk >= tk and q_pos0 % SB_KEY_ALIGN == 0 and tq % SB_KEY_ALIGN == 0, (sk, q_pos0, sq)
    return pl.pallas_call(
        functools.partial(_sb_kernel, q_pos0=q_pos0, tq=tq, tk=tk, width=width),
        grid=(bsz, sq // tq),
        in_specs=[
            pl.BlockSpec((1, tq, d), lambda b, i: (b, i, 0)),
            pl.BlockSpec((1, sk, d), lambda b, i: (b, 0, 0)),
            pl.BlockSpec((1, sk, d), lambda b, i: (b, 0, 0)),
            pl.BlockSpec((tk, tk), lambda b, i: (0, 0)),
        ],
        out_specs=pl.BlockSpec((1, tq, d), lambda b, i: (b, i, 0)),
        out_shape=jax.ShapeDtypeStruct((bsz, sq, d), BF16),
        name="sb_attention",
        compiler_params=_cparams(("arbitrary", "arbitrary")),
    )(q, k_all, v_all, tri)


def _sort_network(n):
    pairs = []
    p = 1
    while p < n:
        k = p
        while k >= 1:
            for j in range(k % p, n - k, 2 * k):
                for i in range(min(k, n - j - k)):
                    if (i + j) // (2 * p) == (i + j + k) // (2 * p):
                        pairs.append((i + j, i + j + k))
            k //= 2
        p *= 2
    return pairs


def _pop_merge(lists, ranks, steps, big, singles=None):
    lists, ranks = list(lists), list(ranks)
    vals, ids = [], []
    for step in range(steps):
        head, head_rank = lists[0], ranks[0]
        if singles is not None:
            head = jnp.concatenate([head, singles[0]], axis=0)
            head_rank = jnp.concatenate([head_rank, singles[1]], axis=0)
        m = jnp.max(head, axis=0, keepdims=True)
        sel = jnp.min(jnp.where(head == m, head_rank, big), axis=0, keepdims=True)
        if singles is not None:
            singles = (jnp.where(singles[1] == sel, -jnp.inf, singles[0]), singles[1])
        vals.append(m)
        ids.append(sel)
        won = ranks[0] == sel
        for r in range(min(steps - 1 - step, len(lists) - 1)):
            lists[r] = jnp.where(won, lists[r + 1], lists[r])
            ranks[r] = jnp.where(won, ranks[r + 1], ranks[r])
    return jnp.concatenate(vals, axis=0), jnp.concatenate(ids, axis=0)


def _top16_rows(s):
    n_slabs = s.shape[0] // 8
    sub = lax.broadcasted_iota(I32, (8, s.shape[1]), 0)
    vals = [s[8 * r:8 * r + 8] for r in range(n_slabs)]
    keys = [sub + 8 * r for r in range(n_slabs)]
    for i, j in _sort_network(n_slabs):
        a, b, ka, kb = vals[i], vals[j], keys[i], keys[j]
        swap = (b > a) | ((b == a) & (kb < ka))
        vals[i], vals[j] = jnp.where(swap, b, a), jnp.where(swap, a, b)
        keys[i], keys[j] = jnp.where(swap, kb, ka), jnp.where(swap, ka, kb)
    return _pop_merge(vals, keys, PEER_TOPK, PEER_N_KEYS)


def _product_top16(v1, n1, v2, n2):
    k = PEER_TOPK
    a_low = lax.broadcasted_iota(I32, (8, v1.shape[1]), 0)
    lists, ranks = [], []
    for b in range(k):
        longest = k // (b + 1)
        lists.append(jnp.where(a_low < longest, v1[0:8] + v2[b:b + 1], -jnp.inf))
        ranks.append(a_low * k + b)
    singles = (v1[8:k] + v2[0:1], (a_low + 8) * k)
    best, code = _pop_merge(lists, ranks, k, k * k, singles)
    rank_a, rank_b = code >> 4, code & (PEER_TOPK - 1)
    bi = jnp.zeros_like(code)
    bj = jnp.zeros_like(code)
    for r in range(PEER_TOPK):
        bi = jnp.where(rank_a == r, n1[r:r + 1], bi)
        bj = jnp.where(rank_b == r, n2[r:r + 1], bj)
    return best, bi, bj


def _route_kernel(h_ref, wq_ref, sk_ref, i_ref, j_ref, w_ref,
                  q_s, v_s, n_s, it_s, jt_s, wt_s):
    tr = h_ref.shape[0]
    nsub = tr // LANES
    x = h_ref[...].astype(BF16)
    q_t = lax.dot_general(wq_ref[...], x, (((1,), (1,)), ((), ())),
                          preferred_element_type=F32)
    q_s[...] = q_t.astype(BF16)

    def half_body(hp, _):
        qs = q_s[pl.ds(pl.multiple_of(hp * PEER_HALF, PEER_HALF), PEER_HALF), :]
        s_t = jnp.dot(sk_ref[hp], qs, preferred_element_type=F32)
        for sub in range(nsub):
            sl = slice(sub * LANES, (sub + 1) * LANES)
            v, n = _top16_rows(s_t[:, sl])
            v_s[hp, :, sl] = v
            n_s[hp, :, sl] = n
        return 0

    lax.fori_loop(0, 2 * PEER_HEADS, half_body, 0)

    def head_body(hd, _):
        for sub in range(nsub):
            sl = slice(sub * LANES, (sub + 1) * LANES)
            v1, v2 = v_s[2 * hd, :, sl], v_s[2 * hd + 1, :, sl]
            n1, n2 = n_s[2 * hd, :, sl], n_s[2 * hd + 1, :, sl]
            best, bi, bj = _product_top16(v1, n1, v2, n2)
            e = jnp.exp(best - best[0:1])
            gate = e / jnp.sum(e, axis=0, keepdims=True)
            rows = pl.ds(pl.multiple_of(hd * PEER_TOPK, PEER_TOPK), PEER_TOPK)
            it_s[rows, sl] = bi
            jt_s[rows, sl] = bj
            wt_s[rows, sl] = gate
        return 0

    lax.fori_loop(0, PEER_HEADS, head_body, 0)
    i_ref[...] = it_s[...].T
    j_ref[...] = jt_s[...].T
    w_ref[...] = wt_s[...].T


def _peer_route(h2, wq_t, sk):
    t, d = h2.shape
    tr = min(t, ROUTE_TILE)
    assert t % tr == 0 and tr % LANES == 0, (t, tr)
    nq = wq_t.shape[0]
    hk = PEER_HEADS * PEER_TOPK
    blk = pl.BlockSpec((tr, hk), lambda i: (i, 0))
    return pl.pallas_call(
        _route_kernel,
        grid=(t // tr,),
        in_specs=[
            pl.BlockSpec((tr, d), lambda i: (i, 0)),
            pl.BlockSpec((nq, d), lambda i: (0, 0)),
            pl.BlockSpec(sk.shape, lambda i: (0, 0, 0)),
        ],
        out_specs=[blk, blk, blk],
        out_shape=[jax.ShapeDtypeStruct((t, hk), I32), jax.ShapeDtypeStruct((t, hk), I32),
                   jax.ShapeDtypeStruct((t, hk), F32)],
        scratch_shapes=[
            pltpu.VMEM((nq, tr), BF16),
            pltpu.VMEM((2 * PEER_HEADS, PEER_TOPK, tr), F32),
            pltpu.VMEM((2 * PEER_HEADS, PEER_TOPK, tr), I32),
            pltpu.VMEM((hk, tr), I32),
            pltpu.VMEM((hk, tr), I32),
            pltpu.VMEM((hk, tr), F32),
        ],
        name="peer_route",
        compiler_params=_cparams(("arbitrary",)),
    )(h2, wq_t, sk)


def _expert_kernel(h_ref, i_ref, j_ref, w_ref, u_ref, v_ref, g_ref, b_ref, o_ref,
                   xb_s, gate_s, acc_s, *, alpha, n_chunks, sub_e):
    c = pl.program_id(1)
    tm = h_ref.shape[0]
    half = tm // 2
    ec = u_ref.shape[0]
    hk = i_ref.shape[1]

    @pl.when(c == 0)
    def _():
        xb_s[...] = h_ref[...].astype(BF16)
        acc_s[...] = jnp.zeros_like(acc_s)
        key = lax.broadcasted_iota(I32, (PEER_N_KEYS, hk), 0)
        zeros = jnp.zeros((PEER_N_KEYS, hk), BF16)

        def group_body(p, _):
            k0 = pl.multiple_of(p * G_PAIRS, G_PAIRS)
            k1 = pl.multiple_of(half + k0, G_PAIRS)
            i_blk = (i_ref[pl.ds(k0, G_PAIRS), :], i_ref[pl.ds(k1, G_PAIRS), :])
            j_blk = (j_ref[pl.ds(k0, G_PAIRS), :], j_ref[pl.ds(k1, G_PAIRS), :])
            w_blk = (w_ref[pl.ds(k0, G_PAIRS), :], w_ref[pl.ds(k1, G_PAIRS), :])
            for pair in range(G_PAIRS):
                lhs, rhs = [], []
                for side in range(2):
                    it = jnp.broadcast_to(i_blk[side][pair:pair + 1, :], (PEER_N_KEYS, hk))
                    jt = jnp.broadcast_to(j_blk[side][pair:pair + 1, :], (PEER_N_KEYS, hk))
                    wt = jnp.broadcast_to(w_blk[side][pair:pair + 1, :], (PEER_N_KEYS, hk))
                    lhs.append(jnp.where(key == it, 1.0, 0.0).astype(BF16))
                    rhs.append(jnp.where(key == jt, wt, 0.0).astype(BF16))
                x_mat = jnp.concatenate(lhs, axis=1)
                y_mat = jnp.concatenate(
                    [jnp.concatenate([rhs[0], zeros], axis=1),
                     jnp.concatenate([zeros, rhs[1]], axis=1)], axis=0)
                gp = lax.dot_general(x_mat, y_mat, (((1,), (1,)), ((), ())),
                                     preferred_element_type=F32)
                base = pl.multiple_of((k0 + pair) * G_PITCH, 8)
                gate_s[pl.ds(base, PEER_N_KEYS), :] = _pack_bf16_pair(gp[:, :LANES], gp[:, LANES:])
            return 0

        lax.fori_loop(0, half // G_PAIRS, group_body, 0)

    i_base = c * (ec // PEER_N_KEYS)
    per_sub = sub_e // PEER_N_KEYS
    for sc in range(ec // sub_e):
        a = lax.dot_general(xb_s[...], u_ref[sc * sub_e:(sc + 1) * sub_e, :], (((1,), (1,)), ((), ())),
                            preferred_element_type=F32)
        gate = jnp.concatenate(
            [jnp.concatenate(_unpack_bf16_pair(
                gate_s[pl.ds(i_base + sc * per_sub + r, half, stride=G_PITCH), :]), axis=0)
             for r in range(per_sub)], axis=1)
        act = 0.5 * a * (1.0 + lax.erf(a * (2.0 ** -0.5)))
        hm = (act * gate).astype(BF16)
        acc_s[...] += jnp.dot(hm, v_ref[sc * sub_e:(sc + 1) * sub_e, :],
                              preferred_element_type=F32)

    @pl.when(c == n_chunks - 1)
    def _():
        o_ref[...] = _layernorm(alpha * h_ref[...] + acc_s[...], g_ref[...], b_ref[...])


def _peer_experts(h2, ri, rj, rw, u_tab, v_tab, g, b, alpha):
    t, d = h2.shape
    n_exp = v_tab.shape[0]
    tm = min(t, EXPERT_TILE)
    ec = EXPERT_CHUNK
    assert t % tm == 0 and tm % (2 * G_PAIRS) == 0 and n_exp % ec == 0, (t, n_exp)
    nc = n_exp // ec
    hk = ri.shape[1]
    tok = pl.BlockSpec((tm, d), lambda i, c: (i, 0))
    rt = pl.BlockSpec((tm, hk), lambda i, c: (i, 0))
    vec = pl.BlockSpec((1, d), lambda i, c: (0, 0))
    return pl.pallas_call(
        functools.partial(_expert_kernel, alpha=alpha, n_chunks=nc, sub_e=EXPERT_SUB),
        grid=(t // tm, nc),
        in_specs=[tok, rt, rt, rt,
                  pl.BlockSpec((ec, d), lambda i, c: (c, 0)),
                  pl.BlockSpec((ec, d), lambda i, c: (c, 0)),
                  vec, vec],
        out_specs=tok,
        out_shape=jax.ShapeDtypeStruct((t, d), F32),
        scratch_shapes=[
            pltpu.VMEM((tm, d), BF16),
            pltpu.VMEM((tm // 2 * G_PITCH, LANES), jnp.uint32),
            pltpu.VMEM((tm, d), F32),
        ],
        name="peer_experts",
        compiler_params=_cparams(("arbitrary", "arbitrary")),
    )(h2, ri, rj, rw, u_tab, v_tab, g, b)


def _trunk(x, conv_prev, k_past, v_past, q_pos0, wts):
    bsz, seq, d = x.shape
    t = bsz * seq
    n_a = wts["conv_w_in"].shape[0]
    depth = n_a + wts["sb_w_q"].shape[0]
    alpha = (2.0 * depth) ** 0.25
    h = x
    new_conv = []
    k_new = v_new = k_all = v_all = None
    for layer in range(depth):
        g0, b0 = wts["ln_g"][layer, 0][None], wts["ln_b"][layer, 0][None]
        g1, b1 = wts["ln_g"][layer, 1][None], wts["ln_b"][layer, 1][None]
        if layer < n_a:
            h, st = _conv_layer(h, conv_prev[layer], wts["conv_w_in"][layer],
                                wts["conv_w_dw"][layer], wts["conv_w_out"][layer], g0, b0, alpha)
            new_conv.append(st)
        else:
            h2 = h.reshape(t, d)
            if layer == n_a:
                k2, v2, kb, vb = _kv_proj(h2, wts["kv_w"])
                k_new = k2.reshape(bsz, seq, d // SB_HEAD_DIM, SB_HEAD_DIM)
                v_new = v2.reshape(bsz, seq, d // SB_HEAD_DIM, SB_HEAD_DIM)
                k_all, v_all = kb.reshape(bsz, seq, d), vb.reshape(bsz, seq, d)
                if k_past is not None:
                    past = k_past.shape[1]
                    k_all = jnp.concatenate([k_past.reshape(bsz, past, d).astype(BF16), k_all], axis=1)
                    v_all = jnp.concatenate([v_past.reshape(bsz, past, d).astype(BF16), v_all], axis=1)
            j = layer - n_a
            q = _q_proj(h2, wts["sb_w_q"][j]).reshape(bsz, seq, d)
            att = _sb_attention(q, k_all, v_all, q_pos0)
            h = _oproj_ln(att.reshape(t, d), h2, wts["sb_w_o"][j], g0, b0, alpha).reshape(bsz, seq, d)
        h2 = h.reshape(t, d)
        ri, rj, rw = _peer_route(h2, wts["peer_wq_t"][layer], wts["peer_sk"][layer])
        h = _peer_experts(h2, ri, rj, rw, wts["peer_u"][layer], wts["peer_v"][layer],
                          g1, b1, alpha).reshape(bsz, seq, d)
    return h, jnp.stack(new_conv, axis=0), k_new, v_new


def kernel(x_prompt, x_sample, state_conv, cache_k, cache_v, conv_w_in, conv_w_dw, conv_w_out,
           sb_w_q, sb_w_o, kv_w_k, kv_w_v, peer_w_q, peer_subkeys, peer_u, peer_v, ln_g, ln_b):
    depth = peer_w_q.shape[0]
    wts = {
        "conv_w_in": conv_w_in.astype(BF16),
        "conv_w_dw": conv_w_dw,
        "conv_w_out": conv_w_out.astype(BF16),
        "sb_w_q": sb_w_q.astype(BF16),
        "sb_w_o": sb_w_o.astype(BF16),
        "kv_w": jnp.concatenate([kv_w_k, kv_w_v], axis=1).astype(BF16),
        "peer_wq_t": jnp.swapaxes(peer_w_q, 1, 2).astype(BF16),
        "peer_sk": peer_subkeys.reshape(depth, 2 * PEER_HEADS, PEER_N_KEYS, PEER_HALF).astype(BF16),
        "peer_u": peer_u.astype(BF16),
        "peer_v": peer_v.astype(BF16),
        "ln_g": ln_g,
        "ln_b": ln_b,
    }
    n_a = conv_w_in.shape[0]
    zero_conv = jnp.zeros((n_a, x_prompt.shape[0], state_conv.shape[2], x_prompt.shape[2]), x_prompt.dtype)
    y_p, conv_p, k_p, v_p = _trunk(x_prompt, zero_conv, None, None, 0, wts)
    y_s, conv_s, k_s, v_s = _trunk(x_sample, state_conv, cache_k, cache_v, cache_k.shape[1], wts)
    return (y_p, y_s, conv_p, k_p, v_p, conv_s, k_s, v_s)
```
